```python
import math
import jax, jax.numpy as jnp
from jax import lax
import numpy as np

D_MODEL = 1024
BATCH = 32
SEQ = 256
DEPTH = 4
DEC_BATCH = 8
DEC_SEQ = 1024
PAST_LEN = 512

GRID_W = 64
CHUNK = 128
Q_BLOCK = 128
EPS = 1e-6

H_SSD = 8
P_SSD = 64
D_SSD = H_SSD * P_SSD
G_SSD = 2
N_SSD = 64
CONV_W = 5
CONV_DIM = D_SSD + 2 * G_SSD * N_SSD
SSD_PROJ = D_SSD + CONV_DIM + 2 * H_SSD

H_RET = 4
DK_RET = 64
DV_RET = 64
D_RET = H_RET * DV_RET
RET_PROJ = 2 * H_RET * DK_RET + 2 * H_RET * DV_RET

H_MLA = 4
Q_RANK = 256
KV_RANK = 128
NOPE_DIM = 64
ROPE_DIM = 32
V_DIM = 64
D_MLA = H_MLA * V_DIM
MLA_PROJ = Q_RANK + KV_RANK + ROPE_DIM
ROPE_BASE = 10000.0

D_MIX = D_SSD + D_RET + D_MLA
D_IN = SSD_PROJ + RET_PROJ + MLA_PROJ
D_FF = -(-8 * D_MODEL // (3 * 256)) * 256

kernel_name = 'hybrid_ssd_retention_mla_diffusion_step'


def _rmsnorm(x, w):
    xf = x.astype(jnp.float32)
    y = xf * lax.rsqrt(jnp.mean(xf * xf, axis=-1, keepdims=True) + EPS)
    return (y * w.astype(jnp.float32)).astype(x.dtype)


def _modulation(cond, w_ada, b_ada):
    m = jax.nn.silu(cond) @ w_ada + b_ada
    return jnp.split(m[:, None, :], 6, axis=-1)


def _dwconv(x, w, b):
    ch = x.shape[-1]
    y = lax.conv_general_dilated(x, w[:, None, :].astype(x.dtype), window_strides=(1,),
                                 padding=[(CONV_W // 2, CONV_W // 2)],
                                 dimension_numbers=('NWC', 'WIO', 'NWC'), feature_group_count=ch)
    return y + b.astype(x.dtype)


def _chunked_scan(q, k, v, log_a, h0):
    f32 = jnp.float32
    b, L, nh, n = q.shape
    p = v.shape[-1]
    nc = L // CHUNK
    qc = q.astype(f32).reshape(b, nc, CHUNK, nh, n)
    kc = k.astype(f32).reshape(b, nc, CHUNK, nh, n)
    vc = v.astype(f32).reshape(b, nc, CHUNK, nh, p)
    cum = lax.cumsum(log_a.astype(f32).reshape(b, nc, CHUNK, nh), axis=2)
    idx = jnp.arange(CHUNK)
    lower = (idx[:, None] >= idx[None, :])[None, None, :, :, None]
    seg = cum[:, :, :, None, :] - cum[:, :, None, :, :]
    decay = jnp.exp(jnp.where(lower, seg, -jnp.inf))
    scores = jnp.einsum('bcihn,bcjhn->bcijh', qc, kc) * decay
    y_intra = jnp.einsum('bcijh,bcjhp->bcihp', scores, vc)
    to_end = jnp.exp(cum[:, :, -1:, :] - cum)
    chunk_state = jnp.einsum('bclhn,bclh,bclhp->bchnp', kc, to_end, vc)
    chunk_decay = jnp.exp(cum[:, :, -1, :])

    def step(h_prev, inp):
        cs, cd = inp
        return h_prev * cd[..., None, None] + cs, h_prev

    h_fin, h_enter = lax.scan(step, h0.astype(f32),
                              (jnp.swapaxes(chunk_state, 0, 1), jnp.swapaxes(chunk_decay, 0, 1)))
    h_enter = jnp.swapaxes(h_enter, 0, 1)
    y_inter = jnp.einsum('bcihn,bchnp,bcih->bcihp', qc, h_enter, jnp.exp(cum))
    y = (y_intra + y_inter).reshape(b, L, nh, p)
    return y.astype(q.dtype), h_fin.astype(h0.dtype)


def _directional_scan(q, k, v, log_a, h0, reverse):
    if reverse:
        q, k, v, log_a = (jnp.flip(t, axis=1) for t in (q, k, v, log_a))
    y, h = _chunked_scan(q, k, v, log_a, h0)
    if reverse:
        y = jnp.flip(y, axis=1)
    return y, h


def _ssd(u, conv_w, conv_b, dt_bias, a_log, d_skip, norm_w, h0):
    b, L, _ = u.shape
    z = u[..., :D_SSD]
    xbc = jax.nn.silu(_dwconv(u[..., D_SSD:D_SSD + CONV_DIM], conv_w, conv_b))
    xs = xbc[..., :D_SSD].reshape(b, L, H_SSD, P_SSD)
    bm = xbc[..., D_SSD:D_SSD + G_SSD * N_SSD].reshape(b, L, G_SSD, N_SSD)
    cm = xbc[..., D_SSD + G_SSD * N_SSD:].reshape(b, L, G_SSD, N_SSD)
    bh = jnp.repeat(bm, H_SSD // G_SSD, axis=2)
    ch = jnp.repeat(cm, H_SSD // G_SSD, axis=2)
    dt = jax.nn.softplus((u[..., D_SSD + CONV_DIM:].reshape(b, L, 2, H_SSD) + dt_bias).astype(jnp.float32))
    a = -jnp.exp(a_log.astype(jnp.float32))
    y_f, h_f = _directional_scan(ch, bh, xs * dt[:, :, 0, :, None], dt[:, :, 0] * a[0], h0[:, 0], False)
    y_b, h_b = _directional_scan(ch, bh, xs * dt[:, :, 1, :, None], dt[:, :, 1] * a[1], h0[:, 1], True)
    y = (y_f + y_b + d_skip[:, None] * xs).reshape(b, L, D_SSD) * jax.nn.silu(z)
    return _rmsnorm(y.astype(u.dtype), norm_w), jnp.stack([h_f, h_b], axis=1)


def _retention(u, decay_logit, gn_w, h0):
    b, L, _ = u.shape
    nq = H_RET * DK_RET
    q = u[..., :nq].reshape(b, L, H_RET, DK_RET)
    k = u[..., nq:2 * nq].reshape(b, L, H_RET, DK_RET) * (DK_RET ** -0.5)
    v = u[..., 2 * nq:2 * nq + D_RET].reshape(b, L, H_RET, DV_RET)
    g = u[..., 2 * nq + D_RET:]
    log_gamma = jax.nn.log_sigmoid(decay_logit.astype(jnp.float32))
    o_f, s_f = _directional_scan(q, k, v, jnp.broadcast_to(log_gamma[0], (b, L, H_RET)), h0[:, 0], False)
    o_b, s_b = _directional_scan(q, k, v, jnp.broadcast_to(log_gamma[1], (b, L, H_RET)), h0[:, 1], True)
    of = (o_f + o_b).astype(jnp.float32)
    mu = jnp.mean(of, axis=-1, keepdims=True)
    var = jnp.mean(jnp.square(of - mu), axis=-1, keepdims=True)
    on = ((of - mu) * lax.rsqrt(var + EPS)).reshape(b, L, D_RET) * gn_w.astype(jnp.float32)
    return (jax.nn.silu(g) * on).astype(u.dtype), jnp.stack([s_f, s_b], axis=1)


def _axial_rope(n_tokens):
    n_rows = n_tokens // GRID_W
    row, col = jnp.meshgrid(jnp.arange(n_rows), jnp.arange(GRID_W), indexing='ij')
    row = row.reshape(-1).astype(jnp.float32)
    col = col.reshape(-1).astype(jnp.float32)
    half = ROPE_DIM // 2
    inv = ROPE_BASE ** (-jnp.arange(0, half, 2, dtype=jnp.float32) / half)
    ang_r = row[:, None] * inv
    ang_c = col[:, None] * inv
    ang = jnp.concatenate([ang_r, ang_r, ang_c, ang_c], axis=-1)
    return jnp.cos(ang), jnp.sin(ang)


def _rot_half(t):
    t1, t2 = jnp.split(t, 2, axis=-1)
    return jnp.concatenate([-t2, t1], axis=-1)


def _apply_rope(x, cos, sin):
    xr, xc = jnp.split(x, 2, axis=-1)
    rot = jnp.concatenate([_rot_half(xr), _rot_half(xc)], axis=-1)
    out = x.astype(jnp.float32) * cos[:, None, :] + rot.astype(jnp.float32) * sin[:, None, :]
    return out.astype(x.dtype)


def _block_attention(q, k, v):
    b, L, nh, dq = q.shape
    nb = L // Q_BLOCK
    qb = jnp.swapaxes(q.reshape(b, nb, Q_BLOCK, nh, dq), 0, 1)
    scale = dq ** -0.5

    def one_block(qblk):
        s = jnp.einsum('bqhd,bkhd->bhqk', qblk, k).astype(jnp.float32) * scale
        p = jax.nn.softmax(s, axis=-1)
        return jnp.einsum('bhqk,bkhd->bqhd', p.astype(v.dtype), v)

    o = lax.map(one_block, qb)
    return jnp.swapaxes(o, 0, 1).reshape(b, L, nh, v.shape[-1])


def _mla(u, q_norm_w, w_uq, kv_norm_w, w_ukv, rope, ctx_kv):
    b, L, _ = u.shape
    q_c = _rmsnorm(u[..., :Q_RANK], q_norm_w)
    ckv = _rmsnorm(u[..., Q_RANK:Q_RANK + KV_RANK], kv_norm_w)
    krope = u[..., Q_RANK + KV_RANK:]
    q = (q_c @ w_uq).reshape(b, L, H_MLA, NOPE_DIM + ROPE_DIM)
    q_nope, q_rope = q[..., :NOPE_DIM], q[..., NOPE_DIM:]
    if ctx_kv is None:
        ckv_all, kr_all = ckv, krope
    else:
        cos, sin = rope
        q_rope = _apply_rope(q_rope, cos, sin)
        kr_lat = _apply_rope(krope[:, :, None, :], cos, sin)[:, :, 0]
        ckv_all = jnp.concatenate([ctx_kv[0].astype(ckv.dtype), ckv], axis=1)
        kr_all = jnp.concatenate([ctx_kv[1].astype(kr_lat.dtype), kr_lat], axis=1)
    s_len = ckv_all.shape[1]
    kv = (ckv_all @ w_ukv).reshape(b, s_len, H_MLA, NOPE_DIM + V_DIM)
    k = jnp.concatenate([kv[..., :NOPE_DIM],
                         jnp.broadcast_to(kr_all[:, :, None, :], (b, s_len, H_MLA, ROPE_DIM)).astype(kv.dtype)], axis=-1)
    v = kv[..., NOPE_DIM:]
    o = _block_attention(jnp.concatenate([q_nope, q_rope], axis=-1), k, v)
    return o.reshape(b, L, D_MLA), ckv, krope


def _layer(x, cond, lp, ctx, rope):
    b = x.shape[0]
    sh1, sc1, g1, sh2, sc2, g2 = _modulation(cond, lp['w_ada'], lp['b_ada'])
    h = _rmsnorm(x, lp['norm1_w']) * (1 + sc1) + sh1
    u = h @ lp['w_in']
    u_ssd = u[..., :SSD_PROJ]
    u_ret = u[..., SSD_PROJ:SSD_PROJ + RET_PROJ]
    u_mla = u[..., SSD_PROJ + RET_PROJ:]
    if ctx is None:
        ssd_h0 = jnp.zeros((b, 2, H_SSD, N_SSD, P_SSD), x.dtype)
        ret_h0 = jnp.zeros((b, 2, H_RET, DK_RET, DV_RET), x.dtype)
        ctx_kv = None
    else:
        ssd_h0, ret_h0, ckv_ctx, kr_ctx = ctx
        ctx_kv = (ckv_ctx, kr_ctx)
    y_ssd, ssd_state = _ssd(u_ssd, lp['ssd_conv_w'], lp['ssd_conv_b'], lp['ssd_dt_bias'], lp['ssd_A_log'],
                            lp['ssd_D'], lp['ssd_norm_w'], ssd_h0)
    y_ret, ret_state = _retention(u_ret, lp['ret_decay_logit'], lp['ret_gn_w'], ret_h0)
    y_mla, ckv, krope = _mla(u_mla, lp['mla_q_norm_w'], lp['mla_w_uq'], lp['mla_kv_norm_w'], lp['mla_w_ukv'],
                             rope, ctx_kv)
    mix = jnp.concatenate([y_ssd.astype(x.dtype), y_ret.astype(x.dtype), y_mla.astype(x.dtype)], axis=-1)
    x = x + g1 * (mix @ lp['w_out'])
    h2 = _rmsnorm(x, lp['norm2_w']) * (1 + sc2) + sh2
    a, gt = jnp.split(h2 @ lp['ffn_w1'], 2, axis=-1)
    x = x + g2 * ((jax.nn.silu(a) * gt) @ lp['ffn_w2'])
    return x, (ssd_state, ret_state, ckv, krope)


def setup_inputs(seed: int = 0) -> dict:
    key = jax.random.key(seed)
    ks = jax.random.split(key, 32)
    f32 = jnp.float32

    def nrm(i, shape, scale):
        return jax.random.normal(ks[i], shape, f32) * scale

    def gain(i, shape):
        return 1.0 + nrm(i, shape, 0.02)

    dt0 = jnp.exp(jax.random.uniform(ks[14], (DEPTH, 2, H_SSD), f32, math.log(1e-3), math.log(1e-1)))
    ret_base = jnp.log(2.0 ** (5.0 + jnp.arange(H_RET, dtype=f32)) - 1.0)
    return {
        'x_prompt': nrm(0, (BATCH, SEQ, D_MODEL), 1.0),
        'x_sample': nrm(1, (DEC_BATCH, DEC_SEQ, D_MODEL), 1.0),
        'c': nrm(2, (DEC_BATCH, D_MODEL), 1.0),
        'state_ssd': nrm(3, (DEC_BATCH, DEPTH, 2, H_SSD, N_SSD, P_SSD), 0.5),
        'state_ret': nrm(4, (DEC_BATCH, DEPTH, 2, H_RET, DK_RET, DV_RET), 0.5),
        'cache_mla_ckv': nrm(5, (DEC_BATCH, DEPTH, PAST_LEN, KV_RANK), 1.0),
        'cache_mla_krope': nrm(6, (DEC_BATCH, DEPTH, PAST_LEN, ROPE_DIM), 1.0),
        'c_ctx': nrm(7, (D_MODEL,), 1.0),
        'w_ada': nrm(8, (DEPTH, D_MODEL, 6 * D_MODEL), 0.5 * D_MODEL ** -0.5),
        'b_ada': nrm(9, (DEPTH, 6 * D_MODEL), 0.01),
        'norm1_w': gain(10, (DEPTH, D_MODEL)),
        'w_in': nrm(11, (DEPTH, D_MODEL, D_IN), D_MODEL ** -0.5),
        'ssd_conv_w': nrm(12, (DEPTH, CONV_W, CONV_DIM), CONV_W ** -0.5),
        'ssd_conv_b': nrm(13, (DEPTH, CONV_DIM), 0.01),
        'ssd_dt_bias': dt0 + jnp.log(-jnp.expm1(-dt0)),
        'ssd_A_log': jnp.log(jax.random.uniform(ks[15], (DEPTH, 2, H_SSD), f32, 1.0, 16.0)),
        'ssd_D': gain(16, (DEPTH, H_SSD)),
        'ssd_norm_w': gain(17, (DEPTH, D_SSD)),
        'ret_decay_logit': ret_base + nrm(18, (DEPTH, 2, H_RET), 0.1),
        'ret_gn_w': gain(19, (DEPTH, D_RET)),
        'mla_q_norm_w': gain(20, (DEPTH, Q_RANK)),
        'mla_w_uq': nrm(21, (DEPTH, Q_RANK, H_MLA * (NOPE_DIM + ROPE_DIM)), Q_RANK ** -0.5),
        'mla_kv_norm_w': gain(22, (DEPTH, KV_RANK)),
        'mla_w_ukv': nrm(23, (DEPTH, KV_RANK, H_MLA * (NOPE_DIM + V_DIM)), KV_RANK ** -0.5),
        'w_out': nrm(24, (DEPTH, D_MIX, D_MODEL), D_MIX ** -0.5),
        'norm2_w': gain(25, (DEPTH, D_MODEL)),
        'ffn_w1': nrm(26, (DEPTH, D_MODEL, 2 * D_FF), D_MODEL ** -0.5),
        'ffn_w2': nrm(27, (DEPTH, D_FF, D_MODEL), D_FF ** -0.5),
        'final_norm_w': gain(28, (D_MODEL,)),
    }


def reference(x_prompt, x_sample, c, state_ssd, state_ret, cache_mla_ckv, cache_mla_krope, c_ctx,
              w_ada, b_ada, norm1_w, w_in, ssd_conv_w, ssd_conv_b, ssd_dt_bias, ssd_A_log, ssd_D, ssd_norm_w,
              ret_decay_logit, ret_gn_w, mla_q_norm_w, mla_w_uq, mla_kv_norm_w, mla_w_ukv, w_out, norm2_w,
              ffn_w1, ffn_w2, final_norm_w):
    stacked = dict(w_ada=w_ada, b_ada=b_ada, norm1_w=norm1_w, w_in=w_in, ssd_conv_w=ssd_conv_w,
                   ssd_conv_b=ssd_conv_b, ssd_dt_bias=ssd_dt_bias, ssd_A_log=ssd_A_log, ssd_D=ssd_D,
                   ssd_norm_w=ssd_norm_w, ret_decay_logit=ret_decay_logit, ret_gn_w=ret_gn_w,
                   mla_q_norm_w=mla_q_norm_w, mla_w_uq=mla_w_uq, mla_kv_norm_w=mla_kv_norm_w,
                   mla_w_ukv=mla_w_ukv, w_out=w_out, norm2_w=norm2_w, ffn_w1=ffn_w1, ffn_w2=ffn_w2)

    xp = x_prompt
    cond_ctx = c_ctx[None, :]
    ssd_list, ret_list, ckv_list, kr_list = [], [], [], []
    for i in range(DEPTH):
        lp = {name: arr[i] for name, arr in stacked.items()}
        xp, (s_ssd, s_ret, ckv, kr) = _layer(xp, cond_ctx, lp, None, None)
        ssd_list.append(s_ssd)
        ret_list.append(s_ret)
        ckv_list.append(ckv)
        kr_list.append(kr)
    y_prompt = _rmsnorm(xp, final_norm_w)

    rope = _axial_rope(x_sample.shape[1])
    xs = x_sample
    for i in range(DEPTH):
        lp = {name: arr[i] for name, arr in stacked.items()}
        ctx = (state_ssd[:, i], state_ret[:, i], cache_mla_ckv[:, i], cache_mla_krope[:, i])
        xs, _ = _layer(xs, c, lp, ctx, rope)
    y_sample = _rmsnorm(xs, final_norm_w)

    return (y_prompt, y_sample, jnp.stack(ssd_list, axis=1), jnp.stack(ret_list, axis=1),
            jnp.stack(ckv_list, axis=1), jnp.stack(kr_list, axis=1))
```

```python
import functools
import math

import jax
import jax.numpy as jnp
from jax import lax
from jax.experimental import pallas as pl
from jax.experimental.pallas import tpu as pltpu

F32 = jnp.float32
BF16 = jnp.bfloat16

D_MODEL = 1024
DEPTH = 4
GRID_W = 64
CHUNK = 128
LANES = 128
EPS = 1e-6

H_SSD, P_SSD, N_SSD, G_SSD = 8, 64, 64, 2
D_SSD = H_SSD * P_SSD
CONV_W = 5
CONV_DIM = D_SSD + 2 * G_SSD * N_SSD
SSD_PROJ = D_SSD + CONV_DIM + 2 * H_SSD
U_SSD = D_SSD + CONV_DIM + LANES

H_RET, DK_RET, DV_RET = 4, 64, 64
D_RET = H_RET * DV_RET
RET_PROJ = 2 * H_RET * DK_RET + 2 * D_RET

H_MLA, Q_RANK, KV_RANK, NOPE_DIM, ROPE_DIM, V_DIM = 4, 256, 128, 64, 32, 64
D_MLA = H_MLA * V_DIM
U_MLA = Q_RANK + KV_RANK + LANES
ROPE_BASE = 10000.0

D_FF = 2816
FF_CHUNK = 1408
TOKEN_TILE = 512
VMEM_LIMIT = 56 * 1024 * 1024


def _dot(a, b):
    return jnp.dot(a, b, preferred_element_type=F32)


def _dot_nt(a, b):
    return lax.dot_general(a, b, (((1,), (1,)), ((), ())), preferred_element_type=F32)


def _dot_exact_rhs(ones_bf16, x):
    hi = x.astype(BF16)
    r1 = x - hi.astype(F32)
    mid = r1.astype(BF16)
    lo = (r1 - mid.astype(F32)).astype(BF16)
    return _dot(ones_bf16, hi) + _dot(ones_bf16, mid) + _dot(ones_bf16, lo)


def _rms(x, w):
    return x * lax.rsqrt(jnp.mean(x * x, axis=-1, keepdims=True) + EPS) * w


def _silu(x):
    return x * jax.nn.sigmoid(x)


def _softplus(x):
    return jnp.maximum(x, 0.0) + jnp.log1p(jnp.exp(-jnp.abs(x)))


def _lane_half(shape):
    return lax.broadcasted_iota(jnp.int32, shape, len(shape) - 1) % LANES < (LANES // 2)


def _pair_cols(m, ia, ib):
    r = m.shape[0]
    a = jnp.broadcast_to(m[:, ia:ia + 1], (r, LANES))
    b = jnp.broadcast_to(m[:, ib:ib + 1], (r, LANES))
    return jnp.where(_lane_half((r, LANES)), a, b)


def _mod_kernel(c_ref, w_ref, b_ref, o_ref):
    s = _silu(c_ref[...]).astype(BF16)
    o_ref[0] = _dot(s, w_ref[0].astype(BF16)) + b_ref[0]


def _modulation(cond, w_ada, b_ada):
    rows = cond.shape[0]
    n = w_ada.shape[-1]
    tn = 1536
    return pl.pallas_call(
        _mod_kernel,
        grid=(DEPTH, n // tn),
        in_specs=[
            pl.BlockSpec((rows, D_MODEL), lambda l, j: (0, 0)),
            pl.BlockSpec((1, D_MODEL, tn), lambda l, j: (l, 0, j)),
            pl.BlockSpec((1, 1, tn), lambda l, j: (l, 0, j)),
        ],
        out_specs=pl.BlockSpec((1, rows, tn), lambda l, j: (l, 0, j)),
        out_shape=jax.ShapeDtypeStruct((DEPTH, rows, n), F32),
        compiler_params=pltpu.CompilerParams(
            dimension_semantics=("arbitrary", "arbitrary"), vmem_limit_bytes=VMEM_LIMIT),
        name="modulation",
    )(cond, w_ada, b_ada.reshape(DEPTH, 1, n))


def _proj_in_kernel(x_ref, mod_ref, n1w_ref, wssd_ref, wret_ref, wmla_ref, ussd_ref, uret_ref, umla_ref):
    sh1 = mod_ref[0, :, 0:D_MODEL]
    sc1 = mod_ref[0, :, D_MODEL:2 * D_MODEL]
    h = (_rms(x_ref[...], n1w_ref[...]) * (1.0 + sc1) + sh1).astype(BF16)
    ussd_ref[...] = _dot(h, wssd_ref[...])
    uret_ref[...] = _dot(h, wret_ref[...])
    umla_ref[...] = _dot(h, wmla_ref[...])


def _proj_in(x, mod, seq_len, n1w, wssd, wret, wmla):
    t = x.shape[0]
    nmod = mod.shape[0]
    tm = min(TOKEN_TILE, seq_len) if nmod > 1 else TOKEN_TILE
    per_seq = seq_len // tm
    mod_map = (lambda i: (i // per_seq, 0, 0)) if nmod > 1 else (lambda i: (0, 0, 0))
    full = lambda i: (0, 0)
    return pl.pallas_call(
        _proj_in_kernel,
        grid=(t // tm,),
        in_specs=[
            pl.BlockSpec((tm, D_MODEL), lambda i: (i, 0)),
            pl.BlockSpec((1, 1, 6 * D_MODEL), mod_map),
            pl.BlockSpec((1, D_MODEL), full),
            pl.BlockSpec((D_MODEL, U_SSD), full),
            pl.BlockSpec((D_MODEL, RET_PROJ), full),
            pl.BlockSpec((D_MODEL, U_MLA), full),
        ],
        out_specs=[
            pl.BlockSpec((tm, U_SSD), lambda i: (i, 0)),
            pl.BlockSpec((tm, RET_PROJ), lambda i: (i, 0)),
            pl.BlockSpec((tm, U_MLA), lambda i: (i, 0)),
        ],
        out_shape=[
            jax.ShapeDtypeStruct((t, U_SSD), F32),
            jax.ShapeDtypeStruct((t, RET_PROJ), F32),
            jax.ShapeDtypeStruct((t, U_MLA), F32),
        ],
        compiler_params=pltpu.CompilerParams(
            dimension_semantics=("arbitrary",), vmem_limit_bytes=VMEM_LIMIT),
        name="proj_in",
    )(x, mod, n1w, wssd, wret, wmla)


def _ssd_kernel(*refs, seq_len, has_h0):
    if has_h0:
        (u_ref, h0_ref, cw_ref, cb_ref, dtb_ref, alog_ref, dsk_ref, nw_ref,
         y_ref, hout_ref, xbc_s, dt_s, la_s, y_s, st_s) = refs
    else:
        (u_ref, cw_ref, cb_ref, dtb_ref, alog_ref, dsk_ref, nw_ref,
         y_ref, hout_ref, xbc_s, dt_s, la_s, y_s, st_s) = refs
    L = seq_len
    nc = L // CHUNK

    xr = u_ref[0, :, D_SSD:D_SSD + CONV_DIM]
    row = lax.broadcasted_iota(jnp.int32, (L, CONV_DIM), 0)
    acc = jnp.zeros((L, CONV_DIM), F32) + cb_ref[...]
    for k in range(CONV_W):
        off = k - CONV_W // 2
        if off == 0:
            sh = xr
        else:
            sh = pltpu.roll(xr, (-off) % L, 0)
            valid = (row + off >= 0) & (row + off < L)
            sh = jnp.where(valid, sh, 0.0)
        acc = acc + sh * cw_ref[k:k + 1, :]
    xbc_s[...] = _silu(acc)

    dt = _softplus(u_ref[0, :, D_SSD + CONV_DIM:U_SSD] + dtb_ref[...])
    dt_s[...] = dt
    la_s[...] = dt * (-jnp.exp(alog_ref[...]))
    y_s[...] = xbc_s[:, 0:D_SSD] * dsk_ref[...]

    if has_h0:
        for d in range(2):
            for h in range(H_SSD):
                g, hh = divmod(h, H_SSD // G_SSD)
                st_s[d, g * N_SSD:(g + 1) * N_SSD, hh * P_SSD:(hh + 1) * P_SSD] = h0_ref[0, 0, d, h]
    else:
        st_s[...] = jnp.zeros(st_s.shape, F32)

    ri = lax.broadcasted_iota(jnp.int32, (CHUNK, CHUNK), 0)
    ci = lax.broadcasted_iota(jnp.int32, (CHUNK, CHUNK), 1)
    first_group_lanes = _lane_half((CHUNK, LANES))
    first_group_rows = lax.broadcasted_iota(jnp.int32, (2 * N_SSD, 4 * P_SSD), 0) < N_SSD
    lane4 = lax.broadcasted_iota(jnp.int32, (2 * N_SSD, 4 * P_SSD), 1) // P_SSD

    def process(start, d):
        start = pl.multiple_of(start, CHUNK)
        rows = pl.ds(start, CHUNK)
        keep = (ri >= ci) if d == 0 else (ri <= ci)
        tri = jnp.where(keep, 1.0, 0.0).astype(BF16)
        la = la_s[rows, :]
        dtc = dt_s[rows, :]
        cs = _dot_exact_rhs(tri, la)
        cst = cs.T
        tot = cs[CHUNK - 1:CHUNK, :] if d == 0 else cs[0:1, :]
        to_end = jnp.exp(tot - cs)
        in_scale = jnp.exp(cs)

        bmat = xbc_s[rows, D_SSD:D_SSD + LANES]
        cmat = xbc_s[rows, D_SSD + LANES:D_SSD + 2 * LANES]
        b_bf = bmat.astype(BF16)
        c_grp = [jnp.where(first_group_lanes, cmat, 0.0).astype(BF16),
                 jnp.where(first_group_lanes, 0.0, cmat).astype(BF16)]
        gmat = [_dot_nt(c_grp[g], b_bf) for g in range(G_SSD)]
        st_bf = st_s[d].astype(BF16)
        y_inter = [_dot(c_grp[g], st_bf) for g in range(G_SSD)]

        vte = []
        for pr in range(H_SSD // 2):
            g = pr // 2
            ia = d * H_SSD + 2 * pr
            ib = ia + 1
            xs = xbc_s[rows, pr * LANES:(pr + 1) * LANES]
            v = xs * _pair_cols(dtc, ia, ib)
            va = jnp.where(first_group_lanes, v, 0.0).astype(BF16)
            vb = jnp.where(first_group_lanes, 0.0, v).astype(BF16)
            seg_a = cs[:, ia:ia + 1] - cst[ia:ia + 1, :]
            seg_b = cs[:, ib:ib + 1] - cst[ib:ib + 1, :]
            s_a = (gmat[g] * jnp.exp(jnp.where(keep, seg_a, -jnp.inf))).astype(BF16)
            s_b = (gmat[g] * jnp.exp(jnp.where(keep, seg_b, -jnp.inf))).astype(BF16)
            y_intra = _dot(s_a, va) + _dot(s_b, vb)
            lo = (pr % 2) * LANES
            y_pair = y_intra + y_inter[g][:, lo:lo + LANES] * _pair_cols(in_scale, ia, ib)
            y_s[rows, pr * LANES:(pr + 1) * LANES] += y_pair
            vte.append((v * _pair_cols(to_end, ia, ib)).astype(BF16))

        bt = bmat.T.astype(BF16)
        new0 = _dot(bt, jnp.concatenate([vte[0], vte[1]], axis=1))
        new1 = _dot(bt, jnp.concatenate([vte[2], vte[3]], axis=1))
        new = jnp.where(first_group_rows, new0, new1)
        cd = jnp.exp(tot)
        decay = jnp.zeros((2 * N_SSD, 4 * P_SSD), F32)
        for h in range(H_SSD):
            g, hh = divmod(h, H_SSD // G_SSD)
            sel = (lane4 == hh) & (first_group_rows if g == 0 else ~first_group_rows)
            decay = jnp.where(sel, cd[:, d * H_SSD + h:d * H_SSD + h + 1], decay)
        st_s[d] = st_s[d] * decay + new

    def body(c, carry):
        process(c * CHUNK, 0)
        process((nc - 1 - c) * CHUNK, 1)
        return carry

    lax.fori_loop(0, nc, body, 0)

    for d in range(2):
        for h in range(H_SSD):
            g, hh = divmod(h, H_SSD // G_SSD)
            hout_ref[0, d, h] = st_s[d, g * N_SSD:(g + 1) * N_SSD, hh * P_SSD:(hh + 1) * P_SSD]

    y = y_s[...] * _silu(u_ref[0, :, 0:D_SSD])
    y_ref[0] = _rms(y, nw_ref[...])


def _ssd(u, seq_len, h0, layer, cw, cb, dtb, alog, dsk, nw):
    nb = u.shape[0] // seq_len
    u3 = u.reshape(nb, seq_len, U_SSD)
    has_h0 = h0 is not None
    full2 = lambda b: (0, 0)
    in_specs = [pl.BlockSpec((1, seq_len, U_SSD), lambda b: (b, 0, 0))]
    args = [u3]
    if has_h0:
        in_specs.append(pl.BlockSpec((1, 1, 2, H_SSD, N_SSD, P_SSD), lambda b: (b, layer, 0, 0, 0, 0)))
        args.append(h0)
    in_specs += [
        pl.BlockSpec((8, CONV_DIM), full2),
        pl.BlockSpec((1, CONV_DIM), full2),
        pl.BlockSpec((1, LANES), full2),
        pl.BlockSpec((1, LANES), full2),
        pl.BlockSpec((1, D_SSD), full2),
        pl.BlockSpec((1, D_SSD), full2),
    ]
    args += [cw, cb, dtb, alog, dsk, nw]
    y, hout = pl.pallas_call(
        functools.partial(_ssd_kernel, seq_len=seq_len, has_h0=has_h0),
        grid=(nb,),
        in_specs=in_specs,
        out_specs=[
            pl.BlockSpec((1, seq_len, D_SSD), lambda b: (b, 0, 0)),
            pl.BlockSpec((1, 2, H_SSD, N_SSD, P_SSD), lambda b: (b, 0, 0, 0, 0)),
        ],
        out_shape=[
            jax.ShapeDtypeStruct((nb, seq_len, D_SSD), F32),
            jax.ShapeDtypeStruct((nb, 2, H_SSD, N_SSD, P_SSD), F32),
        ],
        scratch_shapes=[
            pltpu.VMEM((seq_len, CONV_DIM), F32),
            pltpu.VMEM((seq_len, LANES), F32),
            pltpu.VMEM((seq_len, LANES), F32),
            pltpu.VMEM((seq_len, D_SSD), F32),
            pltpu.VMEM((2, 2 * N_SSD, 4 * P_SSD), F32),
        ],
        compiler_params=pltpu.CompilerParams(
            dimension_semantics=("arbitrary",), vmem_limit_bytes=VMEM_LIMIT),
        name="ssd",
    )(*args)
    return y.reshape(nb * seq_len, D_SSD), hout


def _ret_kernel(*refs, seq_len, has_h0):
    if has_h0:
        u_ref, h0_ref, dl_ref, gnw_ref, y_ref, hout_ref, y_s, st_s = refs
    else:
        u_ref, dl_ref, gnw_ref, y_ref, hout_ref, y_s, st_s = refs
    L = seq_len
    nc = L // CHUNK
    npair = H_RET // 2
    nq = H_RET * DK_RET

    lg_row = -_softplus(-dl_ref[...])
    ri = lax.broadcasted_iota(jnp.int32, (CHUNK, CHUNK), 0)
    ci = lax.broadcasted_iota(jnp.int32, (CHUNK, CHUNK), 1)
    rif = ri.astype(F32)
    half = _lane_half((CHUNK, LANES))
    block_diag = (ri < DK_RET) == (ci < DV_RET)

    if has_h0:
        st_s[...] = jnp.zeros(st_s.shape, F32)
        for d in range(2):
            for h in range(H_RET):
                pr, hh = divmod(h, 2)
                st_s[d, pr, hh * DK_RET:(hh + 1) * DK_RET, hh * DV_RET:(hh + 1) * DV_RET] = h0_ref[0, 0, d, h]
    else:
        st_s[...] = jnp.zeros(st_s.shape, F32)
    y_s[...] = jnp.zeros(y_s.shape, F32)

    def process(start, d):
        start = pl.multiple_of(start, CHUNK)
        rows = pl.ds(start, CHUNK)
        keep = (ri >= ci) if d == 0 else (ri <= ci)
        dist = (ri - ci).astype(F32) if d == 0 else (ci - ri).astype(F32)
        for pr in range(npair):
            lanes = slice(pr * LANES, (pr + 1) * LANES)
            ia = d * H_RET + 2 * pr
            lg_a = lg_row[:, ia:ia + 1]
            lg_b = lg_row[:, ia + 1:ia + 2]
            lg_pair = jnp.where(half, lg_a, lg_b)
            q = u_ref[0, rows, lanes]
            k = u_ref[0, rows, nq + pr * LANES:nq + (pr + 1) * LANES] * (DK_RET ** -0.5)
            v = u_ref[0, rows, 2 * nq + pr * LANES:2 * nq + (pr + 1) * LANES]
            q_bf = q.astype(BF16)
            k_bf = k.astype(BF16)
            qa = jnp.where(half, q, 0.0).astype(BF16)
            qb = jnp.where(half, 0.0, q).astype(BF16)
            va = jnp.where(half, v, 0.0).astype(BF16)
            vb = jnp.where(half, 0.0, v).astype(BF16)
            dec_a = jnp.exp(jnp.where(keep, dist * lg_a, -jnp.inf))
            dec_b = jnp.exp(jnp.where(keep, dist * lg_b, -jnp.inf))
            s_a = (_dot_nt(qa, k_bf) * dec_a).astype(BF16)
            s_b = (_dot_nt(qb, k_bf) * dec_b).astype(BF16)
            y_intra = _dot(s_a, va) + _dot(s_b, vb)
            if d == 0:
                in_scale = jnp.exp((rif + 1.0) * lg_pair)
                to_end = jnp.exp((CHUNK - 1.0 - rif) * lg_pair)
            else:
                in_scale = jnp.exp((CHUNK - rif) * lg_pair)
                to_end = jnp.exp(rif * lg_pair)
            st = st_s[d, pr]
            y_pair = y_intra + _dot(q_bf, st.astype(BF16)) * in_scale
            y_s[rows, lanes] += y_pair
            new = _dot(k.T.astype(BF16), (v * to_end).astype(BF16))
            cd = jnp.exp(float(CHUNK) * lg_pair)
            st_s[d, pr] = st * cd + jnp.where(block_diag, new, 0.0)

    def body(c, carry):
        process(c * CHUNK, 0)
        process((nc - 1 - c) * CHUNK, 1)
        return carry

    lax.fori_loop(0, nc, body, 0)

    for d in range(2):
        for h in range(H_RET):
            pr, hh = divmod(h, 2)
            hout_ref[0, d, h] = st_s[d, pr, hh * DK_RET:(hh + 1) * DK_RET, hh * DV_RET:(hh + 1) * DV_RET]

    half_l = _lane_half((L, LANES))
    for pr in range(npair):
        lanes = slice(pr * LANES, (pr + 1) * LANES)
        o = y_s[:, lanes]
        inv = 1.0 / DV_RET
        sum_a = jnp.sum(jnp.where(half_l, o, 0.0), axis=-1, keepdims=True)
        sum_all = jnp.sum(o, axis=-1, keepdims=True)
        mu = jnp.where(half_l, sum_a, sum_all - sum_a) * inv
        c = o - mu
        c2 = c * c
        sq_a = jnp.sum(jnp.where(half_l, c2, 0.0), axis=-1, keepdims=True)
        sq_all = jnp.sum(c2, axis=-1, keepdims=True)
        var = jnp.where(half_l, sq_a, sq_all - sq_a) * inv
        on = c * lax.rsqrt(var + EPS) * gnw_ref[:, lanes]
        gate = u_ref[0, :, 2 * nq + D_RET + pr * LANES:2 * nq + D_RET + (pr + 1) * LANES]
        y_ref[0, :, lanes] = _silu(gate) * on


def _ret(u, seq_len, h0, layer, dl, gnw):
    nb = u.shape[0] // seq_len
    u3 = u.reshape(nb, seq_len, RET_PROJ)
    has_h0 = h0 is not None
    full2 = lambda b: (0, 0)
    in_specs = [pl.BlockSpec((1, seq_len, RET_PROJ), lambda b: (b, 0, 0))]
    args = [u3]
    if has_h0:
        in_specs.append(pl.BlockSpec((1, 1, 2, H_RET, DK_RET, DV_RET), lambda b: (b, layer, 0, 0, 0, 0)))
        args.append(h0)
    in_specs += [pl.BlockSpec((1, LANES), full2), pl.BlockSpec((1, D_RET), full2)]
    args += [dl, gnw]
    y, hout = pl.pallas_call(
        functools.partial(_ret_kernel, seq_len=seq_len, has_h0=has_h0),
        grid=(nb,),
        in_specs=in_specs,
        out_specs=[
            pl.BlockSpec((1, seq_len, D_RET), lambda b: (b, 0, 0)),
            pl.BlockSpec((1, 2, H_RET, DK_RET, DV_RET), lambda b: (b, 0, 0, 0, 0)),
        ],
        out_shape=[
            jax.ShapeDtypeStruct((nb, seq_len, D_RET), F32),
            jax.ShapeDtypeStruct((nb, 2, H_RET, DK_RET, DV_RET), F32),
        ],
        scratch_shapes=[
            pltpu.VMEM((seq_len, D_RET), F32),
            pltpu.VMEM((2, H_RET // 2, 2 * DK_RET, 2 * DV_RET), F32),
        ],
        compiler_params=pltpu.CompilerParams(
            dimension_semantics=("arbitrary",), vmem_limit_bytes=VMEM_LIMIT),
        name="retention",
    )(*args)
    return y.reshape(nb * seq_len, D_RET), hout


def _rope(x, cos, sin):
    lane = lax.broadcasted_iota(jnp.int32, x.shape, 1)
    up = pltpu.roll(x, LANES - 8, 1)
    down = pltpu.roll(x, 8, 1)
    rot = jnp.where((lane // 8) % 2 == 0, -up, down)
    return x * cos + rot * sin


def _mla_kernel(*refs, seq_len, past_len):
    if past_len:
        (u_ref, pckv_ref, pkr_ref, cos_ref, sin_ref, qnw_ref, kvnw_ref, wq_ref, wk_ref, wv_ref,
         o_ref, q_s, k_s, v_s, ckv_s, kr_s) = refs
    else:
        (u_ref, qnw_ref, kvnw_ref, wq_ref, wk_ref, wv_ref,
         o_ref, ckv_ref, kr_ref, q_s, k_s, v_s, ckv_s, kr_s) = refs
    L = seq_len
    S = past_len + L
    nqb = L // CHUNK
    scale = (NOPE_DIM + ROPE_DIM) ** -0.5

    q_c = _rms(u_ref[0, :, 0:Q_RANK], qnw_ref[...]).astype(BF16)
    ckv = _rms(u_ref[0, :, Q_RANK:Q_RANK + KV_RANK], kvnw_ref[...])
    kr = u_ref[0, :, Q_RANK + KV_RANK:U_MLA]
    if past_len:
        cos = cos_ref[...]
        sin = sin_ref[...]
        ckv_s[0:past_len, :] = pckv_ref[0, 0]
        kr_s[0:past_len, :] = jnp.zeros((past_len, LANES), F32)
        kr_s[0:past_len, 0:ROPE_DIM] = pkr_ref[0, 0]
        kr_s[past_len:S, :] = _rope(kr, cos, sin)
    else:
        ckv_ref[0] = ckv
        kr_ref[0] = kr[:, 0:ROPE_DIM]
        kr_s[...] = kr
    ckv_s[past_len:S, :] = ckv

    ckv_all = ckv_s[...].astype(BF16)
    kr_all = kr_s[...]
    half_s = _lane_half((S, LANES))
    for h in range(H_MLA):
        qh = _dot(q_c, wq_ref[h])
        if past_len:
            qh = _rope(qh, cos, sin)
        q_s[h] = qh.astype(BF16)
        k_s[h] = (_dot(ckv_all, wk_ref[h]) + kr_all).astype(BF16)
    for pr in range(H_MLA // 2):
        vp = _dot(ckv_all, wv_ref[pr])
        v_s[2 * pr] = jnp.where(half_s, vp, 0.0).astype(BF16)
        v_s[2 * pr + 1] = jnp.where(half_s, 0.0, vp).astype(BF16)

    def body(qb, carry):
        rows = pl.ds(pl.multiple_of(qb * CHUNK, CHUNK), CHUNK)
        for pr in range(H_MLA // 2):
            acc = jnp.zeros((CHUNK, LANES), F32)
            for hh in range(2):
                h = 2 * pr + hh
                s = _dot_nt(q_s[h, rows, :], k_s[h]) * scale
                s = s - jnp.max(s, axis=-1, keepdims=True)
                e = jnp.exp(s)
                p = e / jnp.sum(e, axis=-1, keepdims=True)
                acc = acc + _dot(p.astype(BF16), v_s[h])
            o_ref[0, rows, pr * LANES:(pr + 1) * LANES] = acc
        return carry

    lax.fori_loop(0, nqb, body, 0)


def _mla(u, seq_len, cache, layer, rope, qnw, kvnw, wq, wk, wv):
    nb = u.shape[0] // seq_len
    u3 = u.reshape(nb, seq_len, U_MLA)
    past_len = cache[0].shape[2] if cache is not None else 0
    s_len = past_len + seq_len
    full2 = lambda b: (0, 0)
    full3 = lambda b: (0, 0, 0)
    in_specs = [pl.BlockSpec((1, seq_len, U_MLA), lambda b: (b, 0, 0))]
    args = [u3]
    if past_len:
        in_specs += [
            pl.BlockSpec((1, 1, past_len, KV_RANK), lambda b: (b, layer, 0, 0)),
            pl.BlockSpec((1, 1, past_len, ROPE_DIM), lambda b: (b, layer, 0, 0)),
            pl.BlockSpec((seq_len, LANES), full2),
            pl.BlockSpec((seq_len, LANES), full2),
        ]
        args += [cache[0], cache[1], rope[0], rope[1]]
    in_specs += [
        pl.BlockSpec((1, Q_RANK), full2),
        pl.BlockSpec((1, KV_RANK), full2),
        pl.BlockSpec((H_MLA, Q_RANK, LANES), full3),
        pl.BlockSpec((H_MLA, KV_RANK, LANES), full3),
        pl.BlockSpec((H_MLA // 2, KV_RANK, LANES), full3),
    ]
    args += [qnw, kvnw, wq, wk, wv]
    out_specs = [pl.BlockSpec((1, seq_len, D_MLA), lambda b: (b, 0, 0))]
    out_shape = [jax.ShapeDtypeStruct((nb, seq_len, D_MLA), F32)]
    if not past_len:
        out_specs += [
            pl.BlockSpec((1, seq_len, KV_RANK), lambda b: (b, 0, 0)),
            pl.BlockSpec((1, seq_len, ROPE_DIM), lambda b: (b, 0, 0)),
        ]
        out_shape += [
            jax.ShapeDtypeStruct((nb, seq_len, KV_RANK), F32),
            jax.ShapeDtypeStruct((nb, seq_len, ROPE_DIM), F32),
        ]
    outs = pl.pallas_call(
        functools.partial(_mla_kernel, seq_len=seq_len, past_len=past_len),
        grid=(nb,),
        in_specs=in_specs,
        out_specs=out_specs,
        out_shape=out_shape,
        scratch_shapes=[
            pltpu.VMEM((H_MLA, seq_len, LANES), BF16),
            pltpu.VMEM((H_MLA, s_len, LANES), BF16),
            pltpu.VMEM((H_MLA, s_len, LANES), BF16),
            pltpu.VMEM((s_len, KV_RANK), F32),
            pltpu.VMEM((s_len, LANES), F32),
        ],
        compiler_params=pltpu.CompilerParams(
            dimension_semantics=("arbitrary",), vmem_limit_bytes=VMEM_LIMIT),
        name="mla",
    )(*args)
    o = outs[0].reshape(nb * seq_len, D_MLA)
    return (o,) + tuple(outs[1:])


def _out_ffn_kernel(x_ref, mssd_ref, mret_ref, mmla_ref, mod_ref, wos_ref, wor_ref, wom_ref, n2w_ref,
                    w1a_ref, w1g_ref, w2_ref, fnw_ref, o_ref, x1_s, h2_s, acc_s, *, final):
    j = pl.program_id(1)

    @pl.when(j == 0)
    def _():
        mix = (_dot(mssd_ref[...].astype(BF16), wos_ref[...])
               + _dot(mret_ref[...].astype(BF16), wor_ref[...])
               + _dot(mmla_ref[...].astype(BF16), wom_ref[...]))
        g1 = mod_ref[0, :, 2 * D_MODEL:3 * D_MODEL]
        sh2 = mod_ref[0, :, 3 * D_MODEL:4 * D_MODEL]
        sc2 = mod_ref[0, :, 4 * D_MODEL:5 * D_MODEL]
        x1 = x_ref[...] + g1 * mix
        x1_s[...] = x1
        h2_s[...] = (_rms(x1, n2w_ref[...]) * (1.0 + sc2) + sh2).astype(BF16)
        acc_s[...] = jnp.zeros(acc_s.shape, F32)

    h2 = h2_s[...]
    a = _dot(h2, w1a_ref[...])
    gt = _dot(h2, w1g_ref[...])
    acc_s[...] += _dot((_silu(a) * gt).astype(BF16), w2_ref[...])

    @pl.when(j == pl.num_programs(1) - 1)
    def _():
        g2 = mod_ref[0, :, 5 * D_MODEL:6 * D_MODEL]
        out = x1_s[...] + g2 * acc_s[...]
        if final:
            out = _rms(out, fnw_ref[...])
        o_ref[...] = out


def _out_ffn(x, mssd, mret, mmla, mod, seq_len, wos, wor, wom, n2w, w1, w2, fnw, final):
    t = x.shape[0]
    nmod = mod.shape[0]
    tm = min(TOKEN_TILE, seq_len) if nmod > 1 else TOKEN_TILE
    per_seq = seq_len // tm
    nj = D_FF // FF_CHUNK
    mod_map = (lambda i, j: (i // per_seq, 0, 0)) if nmod > 1 else (lambda i, j: (0, 0, 0))
    full = lambda i, j: (0, 0)
    tok = lambda i, j: (i, 0)
    return pl.pallas_call(
        functools.partial(_out_ffn_kernel, final=final),
        grid=(t // tm, nj),
        in_specs=[
            pl.BlockSpec((tm, D_MODEL), tok),
            pl.BlockSpec((tm, D_SSD), tok),
            pl.BlockSpec((tm, D_RET), tok),
            pl.BlockSpec((tm, D_MLA), tok),
            pl.BlockSpec((1, 1, 6 * D_MODEL), mod_map),
            pl.BlockSpec((D_SSD, D_MODEL), full),
            pl.BlockSpec((D_RET, D_MODEL), full),
            pl.BlockSpec((D_MLA, D_MODEL), full),
            pl.BlockSpec((1, D_MODEL), full),
            pl.BlockSpec((D_MODEL, FF_CHUNK), lambda i, j: (0, j)),
            pl.BlockSpec((D_MODEL, FF_CHUNK), lambda i, j: (0, j + nj)),
            pl.BlockSpec((FF_CHUNK, D_MODEL), lambda i, j: (j, 0)),
            pl.BlockSpec((1, D_MODEL), full),
        ],
        out_specs=pl.BlockSpec((tm, D_MODEL), tok),
        out_shape=jax.ShapeDtypeStruct((t, D_MODEL), F32),
        scratch_shapes=[
            pltpu.VMEM((tm, D_MODEL), F32),
            pltpu.VMEM((tm, D_MODEL), BF16),
            pltpu.VMEM((tm, D_MODEL), F32),
        ],
        compiler_params=pltpu.CompilerParams(
            dimension_semantics=("arbitrary", "arbitrary"), vmem_limit_bytes=VMEM_LIMIT),
        name="out_ffn",
    )(x, mssd, mret, mmla, mod, wos, wor, wom, n2w, w1, w1, w2, fnw)


def _pad_lanes(a, width):
    return jnp.pad(a, [(0, 0)] * (a.ndim - 1) + [(0, width - a.shape[-1])])


def _rope_tables(n_tokens):
    n_rows = n_tokens // GRID_W
    row, col = jnp.meshgrid(jnp.arange(n_rows), jnp.arange(GRID_W), indexing='ij')
    row = row.reshape(-1).astype(F32)
    col = col.reshape(-1).astype(F32)
    half = ROPE_DIM // 2
    inv = ROPE_BASE ** (-jnp.arange(0, half, 2, dtype=F32) / half)
    ang_r = row[:, None] * inv
    ang_c = col[:, None] * inv
    ang = jnp.concatenate([ang_r, ang_r, ang_c, ang_c], axis=-1)
    cos = jnp.concatenate([jnp.cos(ang), jnp.ones((n_tokens, LANES - ROPE_DIM), F32)], axis=-1)
    sin = _pad_lanes(jnp.sin(ang), LANES)
    return cos, sin


def kernel(x_prompt, x_sample, c, state_ssd, state_ret, cache_mla_ckv, cache_mla_krope, c_ctx,
           w_ada, b_ada, norm1_w, w_in, ssd_conv_w, ssd_conv_b, ssd_dt_bias, ssd_A_log, ssd_D, ssd_norm_w,
           ret_decay_logit, ret_gn_w, mla_q_norm_w, mla_w_uq, mla_kv_norm_w, mla_w_ukv, w_out, norm2_w,
           ffn_w1, ffn_w2, final_norm_w):
    batch, seq, _ = x_prompt.shape
    dec_batch, dec_seq, _ = x_sample.shape

    ret0 = SSD_PROJ
    mla0 = SSD_PROJ + RET_PROJ
    w_ssd = _pad_lanes(w_in[..., 0:SSD_PROJ], U_SSD).astype(BF16)
    w_ret = w_in[..., ret0:mla0].astype(BF16)
    w_mla = _pad_lanes(w_in[..., mla0:], U_MLA).astype(BF16)
    uq = mla_w_uq.reshape(DEPTH, Q_RANK, H_MLA, NOPE_DIM + ROPE_DIM).transpose(0, 2, 1, 3)
    wq = _pad_lanes(jnp.concatenate([uq[..., NOPE_DIM:], uq[..., :NOPE_DIM]], axis=-1), LANES).astype(BF16)
    ukv = mla_w_ukv.reshape(DEPTH, KV_RANK, H_MLA, NOPE_DIM + V_DIM).transpose(0, 2, 1, 3)
    wk = jnp.pad(ukv[..., :NOPE_DIM], [(0, 0)] * 3 + [(ROPE_DIM, LANES - ROPE_DIM - NOPE_DIM)]).astype(BF16)
    wv = ukv[..., NOPE_DIM:].reshape(DEPTH, H_MLA // 2, 2, KV_RANK, V_DIM)
    wv = wv.transpose(0, 1, 3, 2, 4).reshape(DEPTH, H_MLA // 2, KV_RANK, LANES).astype(BF16)
    wo_ssd = w_out[:, 0:D_SSD].astype(BF16)
    wo_ret = w_out[:, D_SSD:D_SSD + D_RET].astype(BF16)
    wo_mla = w_out[:, D_SSD + D_RET:].astype(BF16)
    w1 = ffn_w1.astype(BF16)
    w2 = ffn_w2.astype(BF16)
    conv_w = jnp.pad(ssd_conv_w, [(0, 0), (0, 8 - CONV_W), (0, 0)])
    conv_b = ssd_conv_b.reshape(DEPTH, 1, CONV_DIM)
    dt_bias = _pad_lanes(ssd_dt_bias.reshape(DEPTH, 1, 2 * H_SSD), LANES)
    a_log = _pad_lanes(ssd_A_log.reshape(DEPTH, 1, 2 * H_SSD), LANES)
    d_skip = jnp.repeat(ssd_D, P_SSD, axis=-1).reshape(DEPTH, 1, D_SSD)
    ssd_nw = ssd_norm_w.reshape(DEPTH, 1, D_SSD)
    decay_logit = _pad_lanes(ret_decay_logit.reshape(DEPTH, 1, 2 * H_RET), LANES)
    gn_w = ret_gn_w.reshape(DEPTH, 1, D_RET)
    q_nw = mla_q_norm_w.reshape(DEPTH, 1, Q_RANK)
    kv_nw = mla_kv_norm_w.reshape(DEPTH, 1, KV_RANK)
    n1w = norm1_w.reshape(DEPTH, 1, D_MODEL)
    n2w = norm2_w.reshape(DEPTH, 1, D_MODEL)
    fnw = final_norm_w.reshape(1, D_MODEL)
    rope = _rope_tables(dec_seq)

    n_cond = 16
    cond = jnp.concatenate([c_ctx[None, :], c, jnp.zeros((n_cond - 1 - dec_batch, D_MODEL), F32)], axis=0)
    mod = _modulation(cond, w_ada, b_ada)

    def layer(i, x, mod_rows, seq_len, h0_ssd, h0_ret, cache, final):
        u_ssd, u_ret, u_mla = _proj_in(x, mod_rows, seq_len, n1w[i], w_ssd[i], w_ret[i], w_mla[i])
        y_ssd, s_ssd = _ssd(u_ssd, seq_len, h0_ssd, i, conv_w[i], conv_b[i], dt_bias[i], a_log[i],
                            d_skip[i], ssd_nw[i])
        y_ret, s_ret = _ret(u_ret, seq_len, h0_ret, i, decay_logit[i], gn_w[i])
        mla_out = _mla(u_mla, seq_len, cache, i, rope, q_nw[i], kv_nw[i], wq[i], wk[i], wv[i])
        x = _out_ffn(x, y_ssd, y_ret, mla_out[0], mod_rows, seq_len, wo_ssd[i], wo_ret[i], wo_mla[i],
                     n2w[i], w1[i], w2[i], fnw, final)
        return x, s_ssd, s_ret, mla_out[1:]

    xp = x_prompt.reshape(batch * seq, D_MODEL)
    ssd_list, ret_list, ckv_list, kr_list = [], [], [], []
    for i in range(DEPTH):
        mod_ctx = mod[i, 0:1].reshape(1, 1, 6 * D_MODEL)
        xp, s_ssd, s_ret, (ckv, kr) = layer(i, xp, mod_ctx, seq, None, None, None, i == DEPTH - 1)
        ssd_list.append(s_ssd)
        ret_list.append(s_ret)
        ckv_list.append(ckv)
        kr_list.append(kr)

    xs = x_sample.reshape(dec_batch * dec_seq, D_MODEL)
    for i in range(DEPTH):
        mod_lat = mod[i, 1:1 + dec_batch].reshape(dec_batch, 1, 6 * D_MODEL)
        xs, _, _, _ = layer(i, xs, mod_lat, dec_seq, state_ssd, state_ret,
                            (cache_mla_ckv, cache_mla_krope), i == DEPTH - 1)

    return (xp.reshape(batch, seq, D_MODEL), xs.reshape(dec_batch, dec_seq, D_MODEL),
            jnp.stack(ssd_list, axis=1), jnp.stack(ret_list, axis=1),
            jnp.stack(ckv_list, axis=1), jnp.stack(kr_list, axis=1))
```

```python
import functools
import math

import jax
import jax.numpy as jnp
from jax import lax
from jax.experimental import pallas as pl
from jax.experimental.pallas import tpu as pltpu

F32 = jnp.float32
BF16 = jnp.bfloat16

D_MODEL = 1024
DEPTH = 4
GRID_W = 64
CHUNK = 128
LANES = 128
SUBLANES = 8
EPS = 1e-6

H_SSD, P_SSD, N_SSD, G_SSD = 8, 64, 64, 2
D_SSD = H_SSD * P_SSD
CONV_W = 5
CONV_DIM = D_SSD + 2 * G_SSD * N_SSD
SSD_PROJ = D_SSD + CONV_DIM + 2 * H_SSD
U_SSD = D_SSD + CONV_DIM + LANES

H_RET, DK_RET, DV_RET = 4, 64, 64
D_RET = H_RET * DV_RET
RET_PROJ = 2 * H_RET * DK_RET + 2 * D_RET

H_MLA, Q_RANK, KV_RANK, NOPE_DIM, ROPE_DIM, V_DIM = 4, 256, 128, 64, 32, 64
D_MLA = H_MLA * V_DIM
U_MLA = Q_RANK + KV_RANK + LANES
ROPE_BASE = 10000.0

D_FF = 2816
FF_CHUNK = 1408
TOKEN_TILE = 512
VMEM_LIMIT = 56 * 1024 * 1024


def _dot(a, b):
    return jnp.dot(a, b, preferred_element_type=F32)


def _dot_nt(a, b):
    return lax.dot_general(a, b, (((1,), (1,)), ((), ())), preferred_element_type=F32)


def _split3(x):
    hi = x.astype(BF16)
    r1 = x - hi.astype(F32)
    mid = r1.astype(BF16)
    lo = (r1 - mid.astype(F32)).astype(BF16)
    return hi, mid, lo


def _dot_exact_rhs(ones_bf16, x):
    hi, mid, lo = _split3(x)
    return _dot(ones_bf16, hi) + _dot(ones_bf16, mid) + _dot(ones_bf16, lo)


def _dot_exact_lhs(x, ones_bf16):
    hi, mid, lo = _split3(x)
    return _dot(hi, ones_bf16) + _dot(mid, ones_bf16) + _dot(lo, ones_bf16)


def _rms(x, w):
    return x * lax.rsqrt(jnp.mean(x * x, axis=-1, keepdims=True) + EPS) * w


def _silu(x):
    return x * jax.nn.sigmoid(x)


def _softplus(x):
    return jnp.maximum(x, 0.0) + jnp.log1p(jnp.exp(-jnp.abs(x)))


def _lane_half(shape):
    return lax.broadcasted_iota(jnp.int32, shape, len(shape) - 1) % LANES < (LANES // 2)


def _mod_kernel(c_ref, w_ref, b_ref, o_ref):
    s = _silu(c_ref[...]).astype(BF16)
    o_ref[0] = _dot(s, w_ref[0].astype(BF16)) + b_ref[0]


def _modulation(cond, w_ada, b_ada):
    rows = cond.shape[0]
    n = w_ada.shape[-1]
    tn = 1536
    return pl.pallas_call(
        _mod_kernel,
        grid=(DEPTH, n // tn),
        in_specs=[
            pl.BlockSpec((rows, D_MODEL), lambda l, j: (0, 0)),
            pl.BlockSpec((1, D_MODEL, tn), lambda l, j: (l, 0, j)),
            pl.BlockSpec((1, 1, tn), lambda l, j: (l, 0, j)),
        ],
        out_specs=pl.BlockSpec((1, rows, tn), lambda l, j: (l, 0, j)),
        out_shape=jax.ShapeDtypeStruct((DEPTH, rows, n), F32),
        compiler_params=pltpu.CompilerParams(
            dimension_semantics=("arbitrary", "arbitrary"), vmem_limit_bytes=VMEM_LIMIT),
        name="modulation",
    )(cond, w_ada, b_ada.reshape(DEPTH, 1, n))


def _proj_in_kernel(x_ref, mod_ref, n1w_ref, wssd_ref, wret_ref, wmla_ref, ussd_ref, uret_ref, umla_ref):
    sh1 = mod_ref[0, :, 0:D_MODEL]
    sc1 = mod_ref[0, :, D_MODEL:2 * D_MODEL]
    h = (_rms(x_ref[...], n1w_ref[...]) * (1.0 + sc1) + sh1).astype(BF16)
    ussd_ref[...] = _dot(h, wssd_ref[...])
    uret_ref[...] = _dot(h, wret_ref[...])
    umla_ref[...] = _dot(h, wmla_ref[...])


def _proj_in(x, mod, seq_len, n1w, wssd, wret, wmla):
    t = x.shape[0]
    nmod = mod.shape[0]
    tm = min(TOKEN_TILE, seq_len) if nmod > 1 else TOKEN_TILE
    per_seq = seq_len // tm
    mod_map = (lambda i: (i // per_seq, 0, 0)) if nmod > 1 else (lambda i: (0, 0, 0))
    full = lambda i: (0, 0)
    return pl.pallas_call(
        _proj_in_kernel,
        grid=(t // tm,),
        in_specs=[
            pl.BlockSpec((tm, D_MODEL), lambda i: (i, 0)),
            pl.BlockSpec((1, 1, 6 * D_MODEL), mod_map),
            pl.BlockSpec((1, D_MODEL), full),
            pl.BlockSpec((D_MODEL, U_SSD), full),
            pl.BlockSpec((D_MODEL, RET_PROJ), full),
            pl.BlockSpec((D_MODEL, U_MLA), full),
        ],
        out_specs=[
            pl.BlockSpec((tm, U_SSD), lambda i: (i, 0)),
            pl.BlockSpec((tm, RET_PROJ), lambda i: (i, 0)),
            pl.BlockSpec((tm, U_MLA), lambda i: (i, 0)),
        ],
        out_shape=[
            jax.ShapeDtypeStruct((t, U_SSD), F32),
            jax.ShapeDtypeStruct((t, RET_PROJ), F32),
            jax.ShapeDtypeStruct((t, U_MLA), F32),
        ],
        compiler_params=pltpu.CompilerParams(
            dimension_semantics=("arbitrary",), vmem_limit_bytes=VMEM_LIMIT),
        name="proj_in",
    )(x, mod, n1w, wssd, wret, wmla)


def _ssd_kernel(*refs, seq_len, has_h0):
    if has_h0:
        (u_ref, h0_ref, cw_ref, cb_ref, dtb_ref, alog_ref, dsk_ref, nw_ref,
         y_ref, hout_ref, xpad_s, xbc_s, dt_s, la_s, y_s, st_s) = refs
    else:
        (u_ref, cw_ref, cb_ref, dtb_ref, alog_ref, dsk_ref, nw_ref,
         y_ref, hout_ref, xpad_s, xbc_s, dt_s, la_s, y_s, st_s) = refs
    L = seq_len
    nc = L // CHUNK

    xpad_s[0:SUBLANES, :] = jnp.zeros((SUBLANES, CONV_DIM), F32)
    xpad_s[SUBLANES + L:2 * SUBLANES + L, :] = jnp.zeros((SUBLANES, CONV_DIM), F32)
    xpad_s[SUBLANES:SUBLANES + L, :] = u_ref[0, :, D_SSD:D_SSD + CONV_DIM]

    for c in range(nc):
        base = c * CHUNK
        acc = jnp.zeros((CHUNK, CONV_DIM), F32) + cb_ref[...]
        for k in range(CONV_W):
            lo = base + SUBLANES + k - CONV_W // 2
            acc = acc + xpad_s[lo:lo + CHUNK, :] * cw_ref[k:k + 1, :]
        xbc_s[base:base + CHUNK, :] = _silu(acc)

    dt = _softplus(u_ref[0, :, D_SSD + CONV_DIM:U_SSD] + dtb_ref[...])
    dt_s[...] = dt
    la_s[...] = dt * (-jnp.exp(alog_ref[...]))
    y_s[...] = xbc_s[:, 0:D_SSD] * dsk_ref[...]

    if has_h0:
        for d in range(2):
            for h in range(H_SSD):
                g, hh = divmod(h, H_SSD // G_SSD)
                st_s[d, g * N_SSD:(g + 1) * N_SSD, hh * P_SSD:(hh + 1) * P_SSD] = h0_ref[0, 0, d, h]
    else:
        st_s[...] = jnp.zeros(st_s.shape, F32)

    ri = lax.broadcasted_iota(jnp.int32, (CHUNK, CHUNK), 0)
    ci = lax.broadcasted_iota(jnp.int32, (CHUNK, CHUNK), 1)
    first_group_lanes = _lane_half((CHUNK, LANES))
    first_group_rows = lax.broadcasted_iota(jnp.int32, (2 * N_SSD, 4 * P_SSD), 0) < N_SSD
    half_row = _lane_half((1, LANES))
    er = lax.broadcasted_iota(jnp.int32, (LANES, D_SSD), 0)
    el = lax.broadcasted_iota(jnp.int32, (LANES, D_SSD), 1) // P_SSD
    expand = [jnp.where(er == d * H_SSD + el, 1.0, 0.0).astype(BF16) for d in range(2)]

    def process(start, d):
        rows = pl.ds(start, CHUNK)
        keep = (ri >= ci) if d == 0 else (ri <= ci)
        tri = jnp.where(keep, 1.0, 0.0).astype(BF16)
        la = la_s[rows, :]
        cs = _dot_exact_rhs(tri, la)
        cst = cs.T
        tot = cs[CHUNK - 1:CHUNK, :] if d == 0 else cs[0:1, :]
        dt_wide = _dot_exact_lhs(dt_s[rows, :], expand[d])
        tot_wide = _dot_exact_lhs(jnp.broadcast_to(tot, (SUBLANES, LANES)), expand[d])[0:1, :]

        bmat = xbc_s[rows, D_SSD:D_SSD + LANES]
        cmat = xbc_s[rows, D_SSD + LANES:D_SSD + 2 * LANES]
        b_bf = bmat.astype(BF16)
        c_grp = [jnp.where(first_group_lanes, cmat, 0.0).astype(BF16),
                 jnp.where(first_group_lanes, 0.0, cmat).astype(BF16)]
        gmat = [_dot_nt(c_grp[g], b_bf) for g in range(G_SSD)]
        st_bf = st_s[d].astype(BF16)
        y_inter = [_dot(c_grp[g], st_bf) for g in range(G_SSD)]

        vte = []
        for pr in range(H_SSD // 2):
            g = pr // 2
            ia = d * H_SSD + 2 * pr
            ib = ia + 1
            pair = slice(pr * LANES, (pr + 1) * LANES)
            v = xbc_s[rows, pair] * dt_wide[:, pair]
            va = jnp.where(first_group_lanes, v, 0.0).astype(BF16)
            vb = jnp.where(first_group_lanes, 0.0, v).astype(BF16)
            col_a = jnp.broadcast_to(cs[:, ia:ia + 1], (CHUNK, LANES))
            col_b = jnp.broadcast_to(cs[:, ib:ib + 1], (CHUNK, LANES))
            seg_a = col_a - cst[ia:ia + 1, :]
            seg_b = col_b - cst[ib:ib + 1, :]
            s_a = (gmat[g] * jnp.exp(jnp.where(keep, seg_a, -jnp.inf))).astype(BF16)
            s_b = (gmat[g] * jnp.exp(jnp.where(keep, seg_b, -jnp.inf))).astype(BF16)
            y_intra = _dot(s_a, va) + _dot(s_b, vb)
            col_pair = jnp.where(first_group_lanes, col_a, col_b)
            tot_pair = jnp.where(half_row, tot[:, ia:ia + 1], tot[:, ib:ib + 1])
            lo = (pr % 2) * LANES
            y_pair = y_intra + y_inter[g][:, lo:lo + LANES] * jnp.exp(col_pair)
            y_s[rows, pair] += y_pair
            vte.append((v * jnp.exp(tot_pair - col_pair)).astype(BF16))

        bt = bmat.T.astype(BF16)
        new0 = _dot(bt, jnp.concatenate([vte[0], vte[1]], axis=1))
        new1 = _dot(bt, jnp.concatenate([vte[2], vte[3]], axis=1))
        new = jnp.where(first_group_rows, new0, new1)
        cd = jnp.exp(tot_wide)
        half_w = 4 * P_SSD
        decay = jnp.where(first_group_rows, cd[:, 0:half_w], cd[:, half_w:2 * half_w])
        st_s[d] = st_s[d] * decay + new

    for c in range(nc):
        process(c * CHUNK, 0)
        process((nc - 1 - c) * CHUNK, 1)

    for d in range(2):
        for h in range(H_SSD):
            g, hh = divmod(h, H_SSD // G_SSD)
            hout_ref[0, d, h] = st_s[d, g * N_SSD:(g + 1) * N_SSD, hh * P_SSD:(hh + 1) * P_SSD]

    y = y_s[...] * _silu(u_ref[0, :, 0:D_SSD])
    y_ref[0] = _rms(y, nw_ref[...])


def _ssd(u, seq_len, h0, layer, cw, cb, dtb, alog, dsk, nw):
    nb = u.shape[0] // seq_len
    u3 = u.reshape(nb, seq_len, U_SSD)
    has_h0 = h0 is not None
    full2 = lambda b: (0, 0)
    in_specs = [pl.BlockSpec((1, seq_len, U_SSD), lambda b: (b, 0, 0))]
    args = [u3]
    if has_h0:
        in_specs.append(pl.BlockSpec((1, 1, 2, H_SSD, N_SSD, P_SSD), lambda b: (b, layer, 0, 0, 0, 0)))
        args.append(h0)
    in_specs += [
        pl.BlockSpec((8, CONV_DIM), full2),
        pl.BlockSpec((1, CONV_DIM), full2),
        pl.BlockSpec((1, LANES), full2),
        pl.BlockSpec((1, LANES), full2),
        pl.BlockSpec((1, D_SSD), full2),
        pl.BlockSpec((1, D_SSD), full2),
    ]
    args += [cw, cb, dtb, alog, dsk, nw]
    y, hout = pl.pallas_call(
        functools.partial(_ssd_kernel, seq_len=seq_len, has_h0=has_h0),
        grid=(nb,),
        in_specs=in_specs,
        out_specs=[
            pl.BlockSpec((1, seq_len, D_SSD), lambda b: (b, 0, 0)),
            pl.BlockSpec((1, 2, H_SSD, N_SSD, P_SSD), lambda b: (b, 0, 0, 0, 0)),
        ],
        out_shape=[
            jax.ShapeDtypeStruct((nb, seq_len, D_SSD), F32),
            jax.ShapeDtypeStruct((nb, 2, H_SSD, N_SSD, P_SSD), F32),
        ],
        scratch_shapes=[
            pltpu.VMEM((seq_len + 2 * SUBLANES, CONV_DIM), F32),
            pltpu.VMEM((seq_len, CONV_DIM), F32),
            pltpu.VMEM((seq_len, LANES), F32),
            pltpu.VMEM((seq_len, LANES), F32),
            pltpu.VMEM((seq_len, D_SSD), F32),
            pltpu.VMEM((2, 2 * N_SSD, 4 * P_SSD), F32),
        ],
        compiler_params=pltpu.CompilerParams(
            dimension_semantics=("arbitrary",), vmem_limit_bytes=VMEM_LIMIT),
        name="ssd",
    )(*args)
    return y.reshape(nb * seq_len, D_SSD), hout


def _ret_kernel(*refs, seq_len, has_h0):
    if has_h0:
        u_ref, h0_ref, dl_ref, gnw_ref, y_ref, hout_ref, y_s, st_s = refs
    else:
        u_ref, dl_ref, gnw_ref, y_ref, hout_ref, y_s, st_s = refs
    L = seq_len
    nc = L // CHUNK
    npair = H_RET // 2
    nq = H_RET * DK_RET

    lg_row = -_softplus(-dl_ref[...])
    ri = lax.broadcasted_iota(jnp.int32, (CHUNK, CHUNK), 0)
    ci = lax.broadcasted_iota(jnp.int32, (CHUNK, CHUNK), 1)
    rif = ri.astype(F32)
    half = _lane_half((CHUNK, LANES))
    block_diag = (ri < DK_RET) == (ci < DV_RET)

    if has_h0:
        st_s[...] = jnp.zeros(st_s.shape, F32)
        for d in range(2):
            for h in range(H_RET):
                pr, hh = divmod(h, 2)
                st_s[d, pr, hh * DK_RET:(hh + 1) * DK_RET, hh * DV_RET:(hh + 1) * DV_RET] = h0_ref[0, 0, d, h]
    else:
        st_s[...] = jnp.zeros(st_s.shape, F32)
    y_s[...] = jnp.zeros(y_s.shape, F32)

    def process(start, d):
        rows = pl.ds(start, CHUNK)
        keep = (ri >= ci) if d == 0 else (ri <= ci)
        dist = (ri - ci).astype(F32) if d == 0 else (ci - ri).astype(F32)
        for pr in range(npair):
            lanes = slice(pr * LANES, (pr + 1) * LANES)
            ia = d * H_RET + 2 * pr
            lg_a = lg_row[:, ia:ia + 1]
            lg_b = lg_row[:, ia + 1:ia + 2]
            lg_pair = jnp.where(half, lg_a, lg_b)
            q = u_ref[0, rows, lanes]
            k = u_ref[0, rows, nq + pr * LANES:nq + (pr + 1) * LANES] * (DK_RET ** -0.5)
            v = u_ref[0, rows, 2 * nq + pr * LANES:2 * nq + (pr + 1) * LANES]
            q_bf = q.astype(BF16)
            k_bf = k.astype(BF16)
            qa = jnp.where(half, q, 0.0).astype(BF16)
            qb = jnp.where(half, 0.0, q).astype(BF16)
            va = jnp.where(half, v, 0.0).astype(BF16)
            vb = jnp.where(half, 0.0, v).astype(BF16)
            dec_a = jnp.exp(jnp.where(keep, dist * lg_a, -jnp.inf))
            dec_b = jnp.exp(jnp.where(keep, dist * lg_b, -jnp.inf))
            s_a = (_dot_nt(qa, k_bf) * dec_a).astype(BF16)
            s_b = (_dot_nt(qb, k_bf) * dec_b).astype(BF16)
            y_intra = _dot(s_a, va) + _dot(s_b, vb)
            if d == 0:
                in_scale = jnp.exp((rif + 1.0) * lg_pair)
                to_end = jnp.exp((CHUNK - 1.0 - rif) * lg_pair)
            else:
                in_scale = jnp.exp((CHUNK - rif) * lg_pair)
                to_end = jnp.exp(rif * lg_pair)
            st = st_s[d, pr]
            y_pair = y_intra + _dot(q_bf, st.astype(BF16)) * in_scale
            y_s[rows, lanes] += y_pair
            new = _dot(k.T.astype(BF16), (v * to_end).astype(BF16))
            cd = jnp.exp(float(CHUNK) * lg_pair)
            st_s[d, pr] = st * cd + jnp.where(block_diag, new, 0.0)

    for c in range(nc):
        process(c * CHUNK, 0)
        process((nc - 1 - c) * CHUNK, 1)

    for d in range(2):
        for h in range(H_RET):
            pr, hh = divmod(h, 2)
            hout_ref[0, d, h] = st_s[d, pr, hh * DK_RET:(hh + 1) * DK_RET, hh * DV_RET:(hh + 1) * DV_RET]

    half_l = _lane_half((L, LANES))
    for pr in range(npair):
        lanes = slice(pr * LANES, (pr + 1) * LANES)
        o = y_s[:, lanes]
        inv = 1.0 / DV_RET
        sum_a = jnp.sum(jnp.where(half_l, o, 0.0), axis=-1, keepdims=True)
        sum_all = jnp.sum(o, axis=-1, keepdims=True)
        mu = jnp.where(half_l, sum_a, sum_all - sum_a) * inv
        c = o - mu
        c2 = c * c
        sq_a = jnp.sum(jnp.where(half_l, c2, 0.0), axis=-1, keepdims=True)
        sq_all = jnp.sum(c2, axis=-1, keepdims=True)
        var = jnp.where(half_l, sq_a, sq_all - sq_a) * inv
        on = c * lax.rsqrt(var + EPS) * gnw_ref[:, lanes]
        gate = u_ref[0, :, 2 * nq + D_RET + pr * LANES:2 * nq + D_RET + (pr + 1) * LANES]
        y_ref[0, :, lanes] = _silu(gate) * on


def _ret(u, seq_len, h0, layer, dl, gnw):
    nb = u.shape[0] // seq_len
    u3 = u.reshape(nb, seq_len, RET_PROJ)
    has_h0 = h0 is not None
    full2 = lambda b: (0, 0)
    in_specs = [pl.BlockSpec((1, seq_len, RET_PROJ), lambda b: (b, 0, 0))]
    args = [u3]
    if has_h0:
        in_specs.append(pl.BlockSpec((1, 1, 2, H_RET, DK_RET, DV_RET), lambda b: (b, layer, 0, 0, 0, 0)))
        args.append(h0)
    in_specs += [pl.BlockSpec((1, LANES), full2), pl.BlockSpec((1, D_RET), full2)]
    args += [dl, gnw]
    y, hout = pl.pallas_call(
        functools.partial(_ret_kernel, seq_len=seq_len, has_h0=has_h0),
        grid=(nb,),
        in_specs=in_specs,
        out_specs=[
            pl.BlockSpec((1, seq_len, D_RET), lambda b: (b, 0, 0)),
            pl.BlockSpec((1, 2, H_RET, DK_RET, DV_RET), lambda b: (b, 0, 0, 0, 0)),
        ],
        out_shape=[
            jax.ShapeDtypeStruct((nb, seq_len, D_RET), F32),
            jax.ShapeDtypeStruct((nb, 2, H_RET, DK_RET, DV_RET), F32),
        ],
        scratch_shapes=[
            pltpu.VMEM((seq_len, D_RET), F32),
            pltpu.VMEM((2, H_RET // 2, 2 * DK_RET, 2 * DV_RET), F32),
        ],
        compiler_params=pltpu.CompilerParams(
            dimension_semantics=("arbitrary",), vmem_limit_bytes=VMEM_LIMIT),
        name="retention",
    )(*args)
    return y.reshape(nb * seq_len, D_RET), hout


def _rot_matrix():
    r = lax.broadcasted_iota(jnp.int32, (LANES, LANES), 0)
    l = lax.broadcasted_iota(jnp.int32, (LANES, LANES), 1)
    even = (l // 8) % 2 == 0
    rope = l < ROPE_DIM
    p = jnp.where(rope & even & (r == l + 8), -1.0, 0.0) + jnp.where(rope & ~even & (r == l - 8), 1.0, 0.0)
    return p.astype(BF16)


def _mla_kernel(*refs, seq_len, past_len):
    if past_len:
        (u_ref, pckv_ref, pkr_ref, cos_ref, sin_ref, qnw_ref, kvnw_ref, wq_ref, wqr_ref, wk_ref, wv_ref,
         o_ref, q_s, k_s, v_s, ckv_s, kr_s) = refs
    else:
        (u_ref, qnw_ref, kvnw_ref, wq_ref, wk_ref, wv_ref,
         o_ref, ckv_ref, kr_ref, q_s, k_s, v_s, ckv_s, kr_s) = refs
    L = seq_len
    S = past_len + L
    nqb = L // CHUNK
    qscale = (NOPE_DIM + ROPE_DIM) ** -0.5 * math.log2(math.e)

    q_c = _rms(u_ref[0, :, 0:Q_RANK], qnw_ref[...]).astype(BF16)
    ckv = _rms(u_ref[0, :, Q_RANK:Q_RANK + KV_RANK], kvnw_ref[...])
    kr = u_ref[0, :, Q_RANK + KV_RANK:U_MLA]
    if past_len:
        cos = cos_ref[...]
        sin = sin_ref[...]
        ckv_s[0:past_len, :] = pckv_ref[0, 0]
        kr_s[0:past_len, :] = jnp.zeros((past_len, LANES), F32)
        kr_s[0:past_len, 0:ROPE_DIM] = pkr_ref[0, 0]
        kr_s[past_len:S, :] = kr * cos + _dot_exact_lhs(kr, _rot_matrix()) * sin
    else:
        ckv_ref[0] = ckv
        kr_ref[0] = kr[:, 0:ROPE_DIM]
        kr_s[...] = kr
    ckv_s[past_len:S, :] = ckv

    ckv_all = ckv_s[...].astype(BF16)
    kr_all = kr_s[...]
    half_s = _lane_half((S, LANES))
    q_all = _dot(q_c, wq_ref[...])
    k_all = _dot(ckv_all, wk_ref[...])
    v_all = _dot(ckv_all, wv_ref[...])
    if past_len:
        qr_all = _dot(q_c, wqr_ref[...])
    for h in range(H_MLA):
        slab = slice(h * LANES, (h + 1) * LANES)
        qh = q_all[:, slab]
        if past_len:
            qh = qh * cos + qr_all[:, slab] * sin
        q_s[h] = (qh * qscale).astype(BF16)
        k_s[h] = (k_all[:, slab] + kr_all).astype(BF16)
    for pr in range(H_MLA // 2):
        vp = v_all[:, pr * LANES:(pr + 1) * LANES]
        v_s[2 * pr] = jnp.where(half_s, vp, 0.0).astype(BF16)
        v_s[2 * pr + 1] = jnp.where(half_s, 0.0, vp).astype(BF16)

    def scores(t):
        qb, h = divmod(t, H_MLA)
        return _dot_nt(q_s[h, qb * CHUNK:(qb + 1) * CHUNK, :], k_s[h])

    ntask = nqb * H_MLA
    s_next = scores(0)
    first = None
    for t in range(ntask):
        qb, h = divmod(t, H_MLA)
        s = s_next
        if t + 1 < ntask:
            s_next = scores(t + 1)
        e = jnp.exp2(s - jnp.max(s, axis=-1, keepdims=True))
        inv = 1.0 / jnp.sum(e, axis=-1, keepdims=True)
        o = _dot(e.astype(BF16), v_s[h]) * inv
        if h % 2 == 0:
            first = o
        else:
            pr = h // 2
            o_ref[0, qb * CHUNK:(qb + 1) * CHUNK, pr * LANES:(pr + 1) * LANES] = first + o


def _mla(u, seq_len, cache, layer, rope, qnw, kvnw, wq, wqr, wk, wv):
    nb = u.shape[0] // seq_len
    u3 = u.reshape(nb, seq_len, U_MLA)
    past_len = cache[0].shape[2] if cache is not None else 0
    s_len = past_len + seq_len
    full2 = lambda b: (0, 0)
    full3 = lambda b: (0, 0, 0)
    in_specs = [pl.BlockSpec((1, seq_len, U_MLA), lambda b: (b, 0, 0))]
    args = [u3]
    if past_len:
        in_specs += [
            pl.BlockSpec((1, 1, past_len, KV_RANK), lambda b: (b, layer, 0, 0)),
            pl.BlockSpec((1, 1, past_len, ROPE_DIM), lambda b: (b, layer, 0, 0)),
            pl.BlockSpec((seq_len, LANES), full2),
            pl.BlockSpec((seq_len, LANES), full2),
        ]
        args += [cache[0], cache[1], rope[0], rope[1]]
    in_specs += [
        pl.BlockSpec((1, Q_RANK), full2),
        pl.BlockSpec((1, KV_RANK), full2),
        pl.BlockSpec((Q_RANK, H_MLA * LANES), full2),
    ]
    args += [qnw, kvnw, wq]
    if past_len:
        in_specs.append(pl.BlockSpec((Q_RANK, H_MLA * LANES), full2))
        args.append(wqr)
    in_specs += [
        pl.BlockSpec((KV_RANK, H_MLA * LANES), full2),
        pl.BlockSpec((KV_RANK, D_MLA), full2),
    ]
    args += [wk, wv]
    out_specs = [pl.BlockSpec((1, seq_len, D_MLA), lambda b: (b, 0, 0))]
    out_shape = [jax.ShapeDtypeStruct((nb, seq_len, D_MLA), F32)]
    if not past_len:
        out_specs += [
            pl.BlockSpec((1, seq_len, KV_RANK), lambda b: (b, 0, 0)),
            pl.BlockSpec((1, seq_len, ROPE_DIM), lambda b: (b, 0, 0)),
        ]
        out_shape += [
            jax.ShapeDtypeStruct((nb, seq_len, KV_RANK), F32),
            jax.ShapeDtypeStruct((nb, seq_len, ROPE_DIM), F32),
        ]
    outs = pl.pallas_call(
        functools.partial(_mla_kernel, seq_len=seq_len, past_len=past_len),
        grid=(nb,),
        in_specs=in_specs,
        out_specs=out_specs,
        out_shape=out_shape,
        scratch_shapes=[
            pltpu.VMEM((H_MLA, seq_len, LANES), BF16),
            pltpu.VMEM((H_MLA, s_len, LANES), BF16),
            pltpu.VMEM((H_MLA, s_len, LANES), BF16),
            pltpu.VMEM((s_len, KV_RANK), F32),
            pltpu.VMEM((s_len, LANES), F32),
        ],
        compiler_params=pltpu.CompilerParams(
            dimension_semantics=("arbitrary",), vmem_limit_bytes=VMEM_LIMIT),
        name="mla",
    )(*args)
    o = outs[0].reshape(nb * seq_len, D_MLA)
    return (o,) + tuple(outs[1:])


def _out_ffn_kernel(x_ref, mssd_ref, mret_ref, mmla_ref, mod_ref, wos_ref, wor_ref, wom_ref, n2w_ref,
                    w1a_ref, w1g_ref, w2_ref, fnw_ref, o_ref, x1_s, h2_s, acc_s, *, final):
    j = pl.program_id(1)

    @pl.when(j == 0)
    def _():
        mix = (_dot(mssd_ref[...].astype(BF16), wos_ref[...])
               + _dot(mret_ref[...].astype(BF16), wor_ref[...])
               + _dot(mmla_ref[...].astype(BF16), wom_ref[...]))
        g1 = mod_ref[0, :, 2 * D_MODEL:3 * D_MODEL]
        sh2 = mod_ref[0, :, 3 * D_MODEL:4 * D_MODEL]
        sc2 = mod_ref[0, :, 4 * D_MODEL:5 * D_MODEL]
        x1 = x_ref[...] + g1 * mix
        x1_s[...] = x1
        h2_s[...] = (_rms(x1, n2w_ref[...]) * (1.0 + sc2) + sh2).astype(BF16)
        acc_s[...] = jnp.zeros(acc_s.shape, F32)

    h2 = h2_s[...]
    a = _dot(h2, w1a_ref[...])
    gt = _dot(h2, w1g_ref[...])
    acc_s[...] += _dot((_silu(a) * gt).astype(BF16), w2_ref[...])

    @pl.when(j == pl.num_programs(1) - 1)
    def _():
        g2 = mod_ref[0, :, 5 * D_MODEL:6 * D_MODEL]
        out = x1_s[...] + g2 * acc_s[...]
        if final:
            out = _rms(out, fnw_ref[...])
        o_ref[...] = out


def _out_ffn(x, mssd, mret, mmla, mod, seq_len, wos, wor, wom, n2w, w1, w2, fnw, final):
    t = x.shape[0]
    nmod = mod.shape[0]
    tm = min(TOKEN_TILE, seq_len) if nmod > 1 else TOKEN_TILE
    per_seq = seq_len // tm
    nj = D_FF // FF_CHUNK
    mod_map = (lambda i, j: (i // per_seq, 0, 0)) if nmod > 1 else (lambda i, j: (0, 0, 0))
    full = lambda i, j: (0, 0)
    tok = lambda i, j: (i, 0)
    return pl.pallas_call(
        functools.partial(_out_ffn_kernel, final=final),
        grid=(t // tm, nj),
        in_specs=[
            pl.BlockSpec((tm, D_MODEL), tok),
            pl.BlockSpec((tm, D_SSD), tok),
            pl.BlockSpec((tm, D_RET), tok),
            pl.BlockSpec((tm, D_MLA), tok),
            pl.BlockSpec((1, 1, 6 * D_MODEL), mod_map),
            pl.BlockSpec((D_SSD, D_MODEL), full),
            pl.BlockSpec((D_RET, D_MODEL), full),
            pl.BlockSpec((D_MLA, D_MODEL), full),
            pl.BlockSpec((1, D_MODEL), full),
            pl.BlockSpec((D_MODEL, FF_CHUNK), lambda i, j: (0, j)),
            pl.BlockSpec((D_MODEL, FF_CHUNK), lambda i, j: (0, j + nj)),
            pl.BlockSpec((FF_CHUNK, D_MODEL), lambda i, j: (j, 0)),
            pl.BlockSpec((1, D_MODEL), full),
        ],
        out_specs=pl.BlockSpec((tm, D_MODEL), tok),
        out_shape=jax.ShapeDtypeStruct((t, D_MODEL), F32),
        scratch_shapes=[
            pltpu.VMEM((tm, D_MODEL), F32),
            pltpu.VMEM((tm, D_MODEL), BF16),
            pltpu.VMEM((tm, D_MODEL), F32),
        ],
        compiler_params=pltpu.CompilerParams(
            dimension_semantics=("arbitrary", "arbitrary"), vmem_limit_bytes=VMEM_LIMIT),
        name="out_ffn",
    )(x, mssd, mret, mmla, mod, wos, wor, wom, n2w, w1, w1, w2, fnw)


def _pad_lanes(a, width):
    return jnp.pad(a, [(0, 0)] * (a.ndim - 1) + [(0, width - a.shape[-1])])


def _rope_tables(n_tokens):
    n_rows = n_tokens // GRID_W
    row, col = jnp.meshgrid(jnp.arange(n_rows), jnp.arange(GRID_W), indexing='ij')
    row = row.reshape(-1).astype(F32)
    col = col.reshape(-1).astype(F32)
    half = ROPE_DIM // 2
    inv = ROPE_BASE ** (-jnp.arange(0, half, 2, dtype=F32) / half)
    ang_r = row[:, None] * inv
    ang_c = col[:, None] * inv
    ang = jnp.concatenate([ang_r, ang_r, ang_c, ang_c], axis=-1)
    cos = jnp.concatenate([jnp.cos(ang), jnp.ones((n_tokens, LANES - ROPE_DIM), F32)], axis=-1)
    sin = _pad_lanes(jnp.sin(ang), LANES)
    return cos, sin


def kernel(x_prompt, x_sample, c, state_ssd, state_ret, cache_mla_ckv, cache_mla_krope, c_ctx,
           w_ada, b_ada, norm1_w, w_in, ssd_conv_w, ssd_conv_b, ssd_dt_bias, ssd_A_log, ssd_D, ssd_norm_w,
           ret_decay_logit, ret_gn_w, mla_q_norm_w, mla_w_uq, mla_kv_norm_w, mla_w_ukv, w_out, norm2_w,
           ffn_w1, ffn_w2, final_norm_w):
    batch, seq, _ = x_prompt.shape
    dec_batch, dec_seq, _ = x_sample.shape

    ret0 = SSD_PROJ
    mla0 = SSD_PROJ + RET_PROJ
    w_ssd = _pad_lanes(w_in[..., 0:SSD_PROJ], U_SSD).astype(BF16)
    w_ret = w_in[..., ret0:mla0].astype(BF16)
    w_mla = _pad_lanes(w_in[..., mla0:], U_MLA).astype(BF16)
    uq = mla_w_uq.reshape(DEPTH, Q_RANK, H_MLA, NOPE_DIM + ROPE_DIM).transpose(0, 2, 1, 3)
    def heads_side_by_side(w):
        return w.transpose(0, 2, 1, 3).reshape(DEPTH, w.shape[2], w.shape[1] * LANES).astype(BF16)

    wq = heads_side_by_side(_pad_lanes(jnp.concatenate([uq[..., NOPE_DIM:], uq[..., :NOPE_DIM]], axis=-1), LANES))
    qr = uq[..., NOPE_DIM:].reshape(DEPTH, H_MLA, Q_RANK, 2, 2, ROPE_DIM // 4)
    wq_rot = jnp.stack([-qr[..., 1, :], qr[..., 0, :]], axis=-2).reshape(DEPTH, H_MLA, Q_RANK, ROPE_DIM)
    wq_rot = heads_side_by_side(_pad_lanes(wq_rot, LANES))
    ukv = mla_w_ukv.reshape(DEPTH, KV_RANK, H_MLA, NOPE_DIM + V_DIM).transpose(0, 2, 1, 3)
    wk = heads_side_by_side(
        jnp.pad(ukv[..., :NOPE_DIM], [(0, 0)] * 3 + [(ROPE_DIM, LANES - ROPE_DIM - NOPE_DIM)]))
    wv = ukv[..., NOPE_DIM:].transpose(0, 2, 1, 3).reshape(DEPTH, KV_RANK, D_MLA).astype(BF16)
    wo_ssd = w_out[:, 0:D_SSD].astype(BF16)
    wo_ret = w_out[:, D_SSD:D_SSD + D_RET].astype(BF16)
    wo_mla = w_out[:, D_SSD + D_RET:].astype(BF16)
    w1 = ffn_w1.astype(BF16)
    w2 = ffn_w2.astype(BF16)
    conv_w = jnp.pad(ssd_conv_w, [(0, 0), (0, 8 - CONV_W), (0, 0)])
    conv_b = ssd_conv_b.reshape(DEPTH, 1, CONV_DIM)
    dt_bias = _pad_lanes(ssd_dt_bias.reshape(DEPTH, 1, 2 * H_SSD), LANES)
    a_log = _pad_lanes(ssd_A_log.reshape(DEPTH, 1, 2 * H_SSD), LANES)
    d_skip = jnp.repeat(ssd_D, P_SSD, axis=-1).reshape(DEPTH, 1, D_SSD)
    ssd_nw = ssd_norm_w.reshape(DEPTH, 1, D_SSD)
    decay_logit = _pad_lanes(ret_decay_logit.reshape(DEPTH, 1, 2 * H_RET), LANES)
    gn_w = ret_gn_w.reshape(DEPTH, 1, D_RET)
    q_nw = mla_q_norm_w.reshape(DEPTH, 1, Q_RANK)
    kv_nw = mla_kv_norm_w.reshape(DEPTH, 1, KV_RANK)
    n1w = norm1_w.reshape(DEPTH, 1, D_MODEL)
    n2w = norm2_w.reshape(DEPTH, 1, D_MODEL)
    fnw = final_norm_w.reshape(1, D_MODEL)
    rope = _rope_tables(dec_seq)

    n_cond = 16
    cond = jnp.concatenate([c_ctx[None, :], c, jnp.zeros((n_cond - 1 - dec_batch, D_MODEL), F32)], axis=0)
    mod = _modulation(cond, w_ada, b_ada)

    def layer(i, x, mod_rows, seq_len, h0_ssd, h0_ret, cache, final):
        u_ssd, u_ret, u_mla = _proj_in(x, mod_rows, seq_len, n1w[i], w_ssd[i], w_ret[i], w_mla[i])
        y_ssd, s_ssd = _ssd(u_ssd, seq_len, h0_ssd, i, conv_w[i], conv_b[i], dt_bias[i], a_log[i],
                            d_skip[i], ssd_nw[i])
        y_ret, s_ret = _ret(u_ret, seq_len, h0_ret, i, decay_logit[i], gn_w[i])
        mla_out = _mla(u_mla, seq_len, cache, i, rope, q_nw[i], kv_nw[i], wq[i], wq_rot[i], wk[i], wv[i])
        x = _out_ffn(x, y_ssd, y_ret, mla_out[0], mod_rows, seq_len, wo_ssd[i], wo_ret[i], wo_mla[i],
                     n2w[i], w1[i], w2[i], fnw, final)
        return x, s_ssd, s_ret, mla_out[1:]

    xp = x_prompt.reshape(batch * seq, D_MODEL)
    ssd_list, ret_list, ckv_list, kr_list = [], [], [], []
    for i in range(DEPTH):
        mod_ctx = mod[i, 0:1].reshape(1, 1, 6 * D_MODEL)
        xp, s_ssd, s_ret, (ckv, kr) = layer(i, xp, mod_ctx, seq, None, None, None, i == DEPTH - 1)
        ssd_list.append(s_ssd)
        ret_list.append(s_ret)
        ckv_list.append(ckv)
        kr_list.append(kr)

    xs = x_sample.reshape(dec_batch * dec_seq, D_MODEL)
    for i in range(DEPTH):
        mod_lat = mod[i, 1:1 + dec_batch].reshape(dec_batch, 1, 6 * D_MODEL)
        xs, _, _, _ = layer(i, xs, mod_lat, dec_seq, state_ssd, state_ret,
                            (cache_mla_ckv, cache_mla_krope), i == DEPTH - 1)

    return (xp.reshape(batch, seq, D_MODEL), xs.reshape(dec_batch, dec_seq, D_MODEL),
            jnp.stack(ssd_list, axis=1), jnp.stack(ret_list, axis=1),
            jnp.stack(ckv_list, axis=1), jnp.stack(kr_list, axis=1))
```

```python
import functools
import math

import jax
import jax.numpy as jnp
from jax import lax
from jax.experimental import pallas as pl
from jax.experimental.pallas import tpu as pltpu

F32 = jnp.float32
BF16 = jnp.bfloat16

D_MODEL = 1024
DEPTH = 4
GRID_W = 64
CHUNK = 128
LANES = 128
SUBLANES = 8
HALO = CHUNK // 2
EPS = 1e-6

H_SSD, P_SSD, N_SSD, G_SSD = 8, 64, 64, 2
D_SSD = H_SSD * P_SSD
CONV_W = 5
CONV_DIM = D_SSD + 2 * G_SSD * N_SSD
SSD_PROJ = D_SSD + CONV_DIM + 2 * H_SSD
U_SSD = D_SSD + CONV_DIM + LANES

H_RET, DK_RET, DV_RET = 4, 64, 64
D_RET = H_RET * DV_RET
RET_PROJ = 2 * H_RET * DK_RET + 2 * D_RET

H_MLA, Q_RANK, KV_RANK, NOPE_DIM, ROPE_DIM, V_DIM = 4, 256, 128, 64, 32, 64
D_MLA = H_MLA * V_DIM
U_MLA = Q_RANK + KV_RANK + LANES
ROPE_BASE = 10000.0

D_FF = 2816
FF_CHUNK = 1408
TOKEN_TILE = 512
N_COND = 16
VMEM_LIMIT = 56 * 1024 * 1024


def _dot(a, b):
    return jnp.dot(a, b, preferred_element_type=F32)


def _dot_nt(a, b):
    return lax.dot_general(a, b, (((1,), (1,)), ((), ())), preferred_element_type=F32)


def _split3(x):
    hi = x.astype(BF16)
    r1 = x - hi.astype(F32)
    mid = r1.astype(BF16)
    lo = (r1 - mid.astype(F32)).astype(BF16)
    return hi, mid, lo


def _dot_exact_rhs(ones_bf16, x):
    hi, mid, lo = _split3(x)
    return _dot(ones_bf16, hi) + _dot(ones_bf16, mid) + _dot(ones_bf16, lo)


def _dot_exact_lhs(x, ones_bf16):
    hi, mid, lo = _split3(x)
    return _dot(hi, ones_bf16) + _dot(mid, ones_bf16) + _dot(lo, ones_bf16)


def _rms(x, w):
    return x * lax.rsqrt(jnp.mean(x * x, axis=-1, keepdims=True) + EPS) * w


def _silu(x):
    h = 0.5 * x
    return h + h * jnp.tanh(h)


def _softplus(x):
    return jnp.maximum(x, 0.0) + jnp.log1p(jnp.exp(-jnp.abs(x)))


def _lane_half(shape):
    return lax.broadcasted_iota(jnp.int32, shape, len(shape) - 1) % LANES < (LANES // 2)


def _mod_kernel(c_ref, w_ref, b_ref, o_ref):
    s = _silu(c_ref[...]).astype(BF16)
    o_ref[0] = _dot(s, w_ref[0].astype(BF16)) + b_ref[0]


def _modulation(cond, w_ada, b_ada):
    rows = cond.shape[0]
    n = w_ada.shape[-1]
    tn = 1536
    return pl.pallas_call(
        _mod_kernel,
        grid=(DEPTH, n // tn),
        in_specs=[
            pl.BlockSpec((rows, D_MODEL), lambda l, j: (0, 0)),
            pl.BlockSpec((1, D_MODEL, tn), lambda l, j: (l, 0, j)),
            pl.BlockSpec((1, 1, tn), lambda l, j: (l, 0, j)),
        ],
        out_specs=pl.BlockSpec((1, rows, tn), lambda l, j: (l, 0, j)),
        out_shape=jax.ShapeDtypeStruct((DEPTH, rows, n), F32),
        compiler_params=pltpu.CompilerParams(
            dimension_semantics=("arbitrary", "arbitrary"), vmem_limit_bytes=VMEM_LIMIT),
        name="modulation",
    )(cond, w_ada, b_ada.reshape(DEPTH, 1, n))


def _proj_in_kernel(x_ref, mod_ref, n1w_ref, wssd_ref, wret_ref, wmla_ref, ussd_ref, uret_ref, umla_ref):
    sh1 = mod_ref[0, :, 0:D_MODEL]
    sc1 = mod_ref[0, :, D_MODEL:2 * D_MODEL]
    h = (_rms(x_ref[...], n1w_ref[...]) * (1.0 + sc1) + sh1).astype(BF16)
    ussd_ref[...] = _dot(h, wssd_ref[...])
    uret_ref[...] = _dot(h, wret_ref[...])
    umla_ref[...] = _dot(h, wmla_ref[...])


def _mod_spec(layer, per_row_tiles):
    base = layer * N_COND
    if per_row_tiles is None:
        index = lambda i, *_: (base, 0, 0)
    else:
        index = lambda i, *_: (base + 1 + i // per_row_tiles, 0, 0)
    return pl.BlockSpec((1, 1, 6 * D_MODEL), index)


def _token_tile(seq_len, shared_cond):
    return TOKEN_TILE if shared_cond else min(TOKEN_TILE, seq_len)


def _proj_in(x, mod, seq_len, shared_cond, layer, n1w, wssd, wret, wmla):
    t = x.shape[0]
    tm = _token_tile(seq_len, shared_cond)
    return pl.pallas_call(
        _proj_in_kernel,
        grid=(t // tm,),
        in_specs=[
            pl.BlockSpec((tm, D_MODEL), lambda i: (i, 0)),
            _mod_spec(layer, None if shared_cond else seq_len // tm),
            _layer_spec(layer, 1, D_MODEL),
            _layer_spec(layer, D_MODEL, U_SSD),
            _layer_spec(layer, D_MODEL, RET_PROJ),
            _layer_spec(layer, D_MODEL, U_MLA),
        ],
        out_specs=[
            pl.BlockSpec((tm, U_SSD), lambda i: (i, 0)),
            pl.BlockSpec((tm, RET_PROJ), lambda i: (i, 0)),
            pl.BlockSpec((tm, U_MLA), lambda i: (i, 0)),
        ],
        out_shape=[
            jax.ShapeDtypeStruct((t, U_SSD), F32),
            jax.ShapeDtypeStruct((t, RET_PROJ), F32),
            jax.ShapeDtypeStruct((t, U_MLA), F32),
        ],
        compiler_params=pltpu.CompilerParams(
            dimension_semantics=("arbitrary",), vmem_limit_bytes=VMEM_LIMIT),
        name="proj_in",
    )(x, mod, n1w, wssd, wret, wmla)


def _ssd_kernel(*refs, seq_len, has_h0, has_prev):
    if has_h0:
        (u_ref, h0_ref, cw_ref, cb_ref, dtb_ref, alog_ref, dsk_ref, nw_ref,
         y_ref, xpad_s, xbc_s, dt_s, la_s, y_s, st_s) = refs
    else:
        refs = refs[:7] + refs[7 + has_prev:]
        (u_ref, cw_ref, cb_ref, dtb_ref, alog_ref, dsk_ref, nw_ref,
         y_ref, hout_ref, xpad_s, xbc_s, dt_s, la_s, y_s, st_s) = refs
    L = seq_len
    nc = L // CHUNK

    xpad_s[0:HALO, :] = jnp.zeros((HALO, CONV_DIM), BF16)
    xpad_s[HALO + L:2 * HALO + L, :] = jnp.zeros((HALO, CONV_DIM), BF16)
    xpad_s[HALO:HALO + L, :] = u_ref[0, :, D_SSD:D_SSD + CONV_DIM].astype(BF16)
    side_taps = [k for k in range(CONV_W) if k != CONV_W // 2]
    sr = lax.broadcasted_iota(jnp.int32, (len(side_taps) * CHUNK, 2 * CHUNK), 0)
    sc = lax.broadcasted_iota(jnp.int32, (len(side_taps) * CHUNK, 2 * CHUNK), 1)
    src_row = sr % CHUNK + HALO
    for i, k in enumerate(side_taps):
        src_row = src_row + jnp.where(sr // CHUNK == i, k - CONV_W // 2, 0)
    shift = jnp.where(sc == src_row, 1.0, 0.0).astype(BF16)

    for c in range(nc):
        base = c * CHUNK
        moved = _dot(shift, xpad_s[base:base + 2 * CHUNK, :])
        centre = u_ref[0, base:base + CHUNK, D_SSD:D_SSD + CONV_DIM]
        acc = cb_ref[...] + centre * cw_ref[CONV_W // 2:CONV_W // 2 + 1, :]
        for i, k in enumerate(side_taps):
            acc = acc + moved[i * CHUNK:(i + 1) * CHUNK, :] * cw_ref[k:k + 1, :]
        xbc_s[base:base + CHUNK, :] = _silu(acc)

    dt = _softplus(u_ref[0, :, D_SSD + CONV_DIM:U_SSD] + dtb_ref[...])
    dt_s[...] = dt
    la_s[...] = dt * (-jnp.exp(alog_ref[...]))
    y_s[...] = xbc_s[:, 0:D_SSD] * dsk_ref[...]

    if has_h0:
        for d in range(2):
            for h in range(H_SSD):
                g, hh = divmod(h, H_SSD // G_SSD)
                st_s[d, g * N_SSD:(g + 1) * N_SSD, hh * P_SSD:(hh + 1) * P_SSD] = h0_ref[0, 0, d, h]
    else:
        st_s[...] = jnp.zeros(st_s.shape, F32)

    ri = lax.broadcasted_iota(jnp.int32, (CHUNK, CHUNK), 0)
    ci = lax.broadcasted_iota(jnp.int32, (CHUNK, CHUNK), 1)
    first_group_lanes = _lane_half((CHUNK, LANES))
    first_group_rows = lax.broadcasted_iota(jnp.int32, (2 * N_SSD, 4 * P_SSD), 0) < N_SSD
    half_row = _lane_half((1, LANES))
    er = lax.broadcasted_iota(jnp.int32, (LANES, D_SSD), 0)
    el = lax.broadcasted_iota(jnp.int32, (LANES, D_SSD), 1) // P_SSD
    expand = [jnp.where(er == d * H_SSD + el, 1.0, 0.0).astype(BF16) for d in range(2)]

    keeps = [ri >= ci, ri <= ci]
    tris = [jnp.where(kp, 1.0, 0.0).astype(BF16) for kp in keeps]

    def stage_a(start, d):
        rows = pl.ds(start, CHUNK)
        cs = _dot_exact_rhs(tris[d], la_s[rows, :])
        tot = cs[CHUNK - 1:CHUNK, :] if d == 0 else cs[0:1, :]
        dt_wide = _dot_exact_lhs(dt_s[rows, :], expand[d])
        tot_wide = _dot_exact_lhs(jnp.broadcast_to(tot, (SUBLANES, LANES)), expand[d])[0:1, :]
        bmat = xbc_s[rows, D_SSD:D_SSD + LANES]
        cmat = xbc_s[rows, D_SSD + LANES:D_SSD + 2 * LANES]
        b_bf = bmat.astype(BF16)
        c_grp = [jnp.where(first_group_lanes, cmat, 0.0).astype(BF16),
                 jnp.where(first_group_lanes, 0.0, cmat).astype(BF16)]
        gmat = [_dot_nt(c_grp[g], b_bf) for g in range(G_SSD)]
        st_bf = st_s[d].astype(BF16)
        y_inter = [_dot(c_grp[g], st_bf) for g in range(G_SSD)]
        return dict(rows=rows, d=d, cs=cs, cst=cs.T, tot=tot, dt_wide=dt_wide, tot_wide=tot_wide,
                    gmat=gmat, y_inter=y_inter, bt=bmat.T.astype(BF16))

    def stage_b(t):
        rows, d, cs, cst, tot, dt_wide = t["rows"], t["d"], t["cs"], t["cst"], t["tot"], t["dt_wide"]
        gmat, y_inter, keep = t["gmat"], t["y_inter"], keeps[t["d"]]
        vte = []
        for pr in range(H_SSD // 2):
            g = pr // 2
            ia = d * H_SSD + 2 * pr
            ib = ia + 1
            pair = slice(pr * LANES, (pr + 1) * LANES)
            v = xbc_s[rows, pair] * dt_wide[:, pair]
            va = jnp.where(first_group_lanes, v, 0.0).astype(BF16)
            vb = jnp.where(first_group_lanes, 0.0, v).astype(BF16)
            col_a = jnp.broadcast_to(cs[:, ia:ia + 1], (CHUNK, LANES))
            col_b = jnp.broadcast_to(cs[:, ib:ib + 1], (CHUNK, LANES))
            seg_a = col_a - cst[ia:ia + 1, :]
            seg_b = col_b - cst[ib:ib + 1, :]
            s_a = (gmat[g] * jnp.exp(jnp.where(keep, seg_a, -jnp.inf))).astype(BF16)
            s_b = (gmat[g] * jnp.exp(jnp.where(keep, seg_b, -jnp.inf))).astype(BF16)
            y_intra = _dot(s_a, va) + _dot(s_b, vb)
            col_pair = jnp.where(first_group_lanes, col_a, col_b)
            tot_pair = jnp.where(half_row, tot[:, ia:ia + 1], tot[:, ib:ib + 1])
            lo = (pr % 2) * LANES
            y_pair = y_intra + y_inter[g][:, lo:lo + LANES] * jnp.exp(col_pair)
            y_s[rows, pair] += y_pair
            vte.append((v * jnp.exp(tot_pair - col_pair)).astype(BF16))

        bt = t["bt"]
        new0 = _dot(bt, jnp.concatenate([vte[0], vte[1]], axis=1))
        new1 = _dot(bt, jnp.concatenate([vte[2], vte[3]], axis=1))
        new = jnp.where(first_group_rows, new0, new1)
        cd = jnp.exp(t["tot_wide"])
        half_w = 4 * P_SSD
        decay = jnp.where(first_group_rows, cd[:, 0:half_w], cd[:, half_w:2 * half_w])
        st_s[d] = st_s[d] * decay + new

    tasks = [((c if d == 0 else nc - 1 - c) * CHUNK, d) for c in range(nc) for d in range(2)]
    pending = stage_a(*tasks[0])
    for nxt in tasks[1:]:
        upcoming = stage_a(*nxt)
        stage_b(pending)
        pending = upcoming
    stage_b(pending)

    if not has_h0:
        for d in range(2):
            for h in range(H_SSD):
                g, hh = divmod(h, H_SSD // G_SSD)
                hout_ref[0, d, h] = st_s[d, g * N_SSD:(g + 1) * N_SSD, hh * P_SSD:(hh + 1) * P_SSD]

    y = y_s[...] * _silu(u_ref[0, :, 0:D_SSD])
    y_ref[0] = _rms(y, nw_ref[...])


def _layer_spec(layer, *tail):
    zeros = (0,) * len(tail)
    return pl.BlockSpec((None,) + tail, lambda *_: (layer,) + zeros)


def _state_out(nb, layer, prev, *tail):
    zeros = (0,) * len(tail)
    spec = pl.BlockSpec((1, None) + tail, lambda b: (b, layer) + zeros)
    shape = jax.ShapeDtypeStruct((nb, DEPTH) + tail, F32)
    return spec, shape, ([] if prev is None else [prev])


def _ssd(u, seq_len, h0, layer, prev_states, cw, cb, dtb, alog, dsk, nw):
    nb = u.shape[0] // seq_len
    u3 = u.reshape(nb, seq_len, U_SSD)
    has_h0 = h0 is not None
    in_specs = [pl.BlockSpec((1, seq_len, U_SSD), lambda b: (b, 0, 0))]
    args = [u3]
    if has_h0:
        in_specs.append(pl.BlockSpec((1, 1, 2, H_SSD, N_SSD, P_SSD), lambda b: (b, layer, 0, 0, 0, 0)))
        args.append(h0)
    in_specs += [
        _layer_spec(layer, 8, CONV_DIM),
        _layer_spec(layer, 1, CONV_DIM),
        _layer_spec(layer, 1, LANES),
        _layer_spec(layer, 1, LANES),
        _layer_spec(layer, 1, D_SSD),
        _layer_spec(layer, 1, D_SSD),
    ]
    args += [cw, cb, dtb, alog, dsk, nw]
    out_specs = [pl.BlockSpec((1, seq_len, D_SSD), lambda b: (b, 0, 0))]
    out_shape = [jax.ShapeDtypeStruct((nb, seq_len, D_SSD), F32)]
    aliases = {}
    if not has_h0:
        spec, shape, prev = _state_out(nb, layer, prev_states, 2, H_SSD, N_SSD, P_SSD)
        out_specs.append(spec)
        out_shape.append(shape)
        if prev:
            aliases = {len(args): 1}
            in_specs.append(pl.BlockSpec(memory_space=pl.ANY))
            args += prev
    outs = pl.pallas_call(
        functools.partial(_ssd_kernel, seq_len=seq_len, has_h0=has_h0, has_prev=bool(aliases)),
        grid=(nb,),
        in_specs=in_specs,
        out_specs=out_specs,
        out_shape=out_shape,
        input_output_aliases=aliases,
        scratch_shapes=[
            pltpu.VMEM((seq_len + 2 * HALO, CONV_DIM), BF16),
            pltpu.VMEM((seq_len, CONV_DIM), F32),
            pltpu.VMEM((seq_len, LANES), F32),
            pltpu.VMEM((seq_len, LANES), F32),
            pltpu.VMEM((seq_len, D_SSD), F32),
            pltpu.VMEM((2, 2 * N_SSD, 4 * P_SSD), F32),
        ],
        compiler_params=pltpu.CompilerParams(
            dimension_semantics=("arbitrary",), vmem_limit_bytes=VMEM_LIMIT),
        name="ssd",
    )(*args)
    return outs[0].reshape(nb * seq_len, D_SSD), (None if has_h0 else outs[1])


def _ret_kernel(*refs, seq_len, has_h0, has_prev):
    if has_h0:
        u_ref, h0_ref, dl_ref, gnw_ref, y_ref, y_s, st_s = refs
    else:
        refs = refs[:3] + refs[3 + has_prev:]
        u_ref, dl_ref, gnw_ref, y_ref, hout_ref, y_s, st_s = refs
    L = seq_len
    nc = L // CHUNK
    npair = H_RET // 2
    nq = H_RET * DK_RET

    lg_row = -_softplus(-dl_ref[...])
    ri = lax.broadcasted_iota(jnp.int32, (CHUNK, CHUNK), 0)
    ci = lax.broadcasted_iota(jnp.int32, (CHUNK, CHUNK), 1)
    rif = ri.astype(F32)
    half = _lane_half((CHUNK, LANES))
    block_diag = (ri < DK_RET) == (ci < DV_RET)

    if has_h0:
        st_s[...] = jnp.zeros(st_s.shape, F32)
        for d in range(2):
            for h in range(H_RET):
                pr, hh = divmod(h, 2)
                st_s[d, pr, hh * DK_RET:(hh + 1) * DK_RET, hh * DV_RET:(hh + 1) * DV_RET] = h0_ref[0, 0, d, h]
    else:
        st_s[...] = jnp.zeros(st_s.shape, F32)
    y_s[...] = jnp.zeros(y_s.shape, F32)

    consts = {}
    for d in range(2):
        keep = (ri >= ci) if d == 0 else (ri <= ci)
        dist = (ri - ci).astype(F32) if d == 0 else (ci - ri).astype(F32)
        for pr in range(npair):
            ia = d * H_RET + 2 * pr
            lg_a = lg_row[:, ia:ia + 1]
            lg_b = lg_row[:, ia + 1:ia + 2]
            lg_pair = jnp.where(half, lg_a, lg_b)
            if d == 0:
                in_scale = jnp.exp((rif + 1.0) * lg_pair)
                to_end = jnp.exp((CHUNK - 1.0 - rif) * lg_pair)
            else:
                in_scale = jnp.exp((CHUNK - rif) * lg_pair)
                to_end = jnp.exp(rif * lg_pair)
            consts[d, pr] = dict(
                dec_a=jnp.exp(jnp.where(keep, dist * lg_a, -jnp.inf)),
                dec_b=jnp.exp(jnp.where(keep, dist * lg_b, -jnp.inf)),
                in_scale=in_scale, to_end=to_end, cd=jnp.exp(float(CHUNK) * lg_pair))

    def stage_a(start, d, pr):
        cst = consts[d, pr]
        rows = pl.ds(start, CHUNK)
        lanes = slice(pr * LANES, (pr + 1) * LANES)
        q = u_ref[0, rows, lanes]
        k = u_ref[0, rows, nq + pr * LANES:nq + (pr + 1) * LANES] * (DK_RET ** -0.5)
        v = u_ref[0, rows, 2 * nq + pr * LANES:2 * nq + (pr + 1) * LANES]
        k_bf = k.astype(BF16)
        qk_a = _dot_nt(jnp.where(half, q, 0.0).astype(BF16), k_bf)
        qk_b = _dot_nt(jnp.where(half, 0.0, q).astype(BF16), k_bf)
        st = st_s[d, pr]
        y_inter = _dot(q.astype(BF16), st.astype(BF16)) * cst["in_scale"]
        new = _dot(k.T.astype(BF16), (v * cst["to_end"]).astype(BF16))
        st_s[d, pr] = st * cst["cd"] + jnp.where(block_diag, new, 0.0)
        return dict(rows=rows, lanes=lanes, cst=cst, qk_a=qk_a, qk_b=qk_b, y_inter=y_inter,
                    va=jnp.where(half, v, 0.0).astype(BF16), vb=jnp.where(half, 0.0, v).astype(BF16))

    def stage_b(t):
        s_a = (t["qk_a"] * t["cst"]["dec_a"]).astype(BF16)
        s_b = (t["qk_b"] * t["cst"]["dec_b"]).astype(BF16)
        y_s[t["rows"], t["lanes"]] += _dot(s_a, t["va"]) + _dot(s_b, t["vb"]) + t["y_inter"]

    tasks = [((c if d == 0 else nc - 1 - c) * CHUNK, d, pr)
             for c in range(nc) for d in range(2) for pr in range(npair)]
    pending = stage_a(*tasks[0])
    for nxt in tasks[1:]:
        upcoming = stage_a(*nxt)
        stage_b(pending)
        pending = upcoming
    stage_b(pending)

    if not has_h0:
        for d in range(2):
            for h in range(H_RET):
                pr, hh = divmod(h, 2)
                hout_ref[0, d, h] = st_s[d, pr, hh * DK_RET:(hh + 1) * DK_RET, hh * DV_RET:(hh + 1) * DV_RET]

    half_l = _lane_half((L, LANES))
    for pr in range(npair):
        lanes = slice(pr * LANES, (pr + 1) * LANES)
        o = y_s[:, lanes]
        inv = 1.0 / DV_RET
        sum_a = jnp.sum(jnp.where(half_l, o, 0.0), axis=-1, keepdims=True)
        sum_all = jnp.sum(o, axis=-1, keepdims=True)
        mu = jnp.where(half_l, sum_a, sum_all - sum_a) * inv
        c = o - mu
        c2 = c * c
        sq_a = jnp.sum(jnp.where(half_l, c2, 0.0), axis=-1, keepdims=True)
        sq_all = jnp.sum(c2, axis=-1, keepdims=True)
        var = jnp.where(half_l, sq_a, sq_all - sq_a) * inv
        on = c * lax.rsqrt(var + EPS) * gnw_ref[:, lanes]
        gate = u_ref[0, :, 2 * nq + D_RET + pr * LANES:2 * nq + D_RET + (pr + 1) * LANES]
        y_ref[0, :, lanes] = _silu(gate) * on


def _ret(u, seq_len, h0, layer, prev_states, dl, gnw):
    nb = u.shape[0] // seq_len
    u3 = u.reshape(nb, seq_len, RET_PROJ)
    has_h0 = h0 is not None
    in_specs = [pl.BlockSpec((1, seq_len, RET_PROJ), lambda b: (b, 0, 0))]
    args = [u3]
    if has_h0:
        in_specs.append(pl.BlockSpec((1, 1, 2, H_RET, DK_RET, DV_RET), lambda b: (b, layer, 0, 0, 0, 0)))
        args.append(h0)
    in_specs += [_layer_spec(layer, 1, LANES), _layer_spec(layer, 1, D_RET)]
    args += [dl, gnw]
    out_specs = [pl.BlockSpec((1, seq_len, D_RET), lambda b: (b, 0, 0))]
    out_shape = [jax.ShapeDtypeStruct((nb, seq_len, D_RET), F32)]
    aliases = {}
    if not has_h0:
        spec, shape, prev = _state_out(nb, layer, prev_states, 2, H_RET, DK_RET, DV_RET)
        out_specs.append(spec)
        out_shape.append(shape)
        if prev:
            aliases = {len(args): 1}
            in_specs.append(pl.BlockSpec(memory_space=pl.ANY))
            args += prev
    outs = pl.pallas_call(
        functools.partial(_ret_kernel, seq_len=seq_len, has_h0=has_h0, has_prev=bool(aliases)),
        grid=(nb,),
        in_specs=in_specs,
        out_specs=out_specs,
        out_shape=out_shape,
        input_output_aliases=aliases,
        scratch_shapes=[
            pltpu.VMEM((seq_len, D_RET), F32),
            pltpu.VMEM((2, H_RET // 2, 2 * DK_RET, 2 * DV_RET), F32),
        ],
        compiler_params=pltpu.CompilerParams(
            dimension_semantics=("arbitrary",), vmem_limit_bytes=VMEM_LIMIT),
        name="retention",
    )(*args)
    return outs[0].reshape(nb * seq_len, D_RET), (None if has_h0 else outs[1])


def _rot_matrix():
    r = lax.broadcasted_iota(jnp.int32, (LANES, LANES), 0)
    l = lax.broadcasted_iota(jnp.int32, (LANES, LANES), 1)
    even = (l // 8) % 2 == 0
    rope = l < ROPE_DIM
    p = jnp.where(rope & even & (r == l + 8), -1.0, 0.0) + jnp.where(rope & ~even & (r == l - 8), 1.0, 0.0)
    return p.astype(BF16)


def _mla_kernel(*refs, seq_len, past_len, n_prev):
    if past_len:
        (u_ref, pckv_ref, pkr_ref, cos_ref, sin_ref, qnw_ref, kvnw_ref, wq_ref, wqr_ref, wk_ref, wv_ref,
         o_ref, q_s, k_s, v_s, ckv_s, kr_s) = refs
    else:
        refs = refs[:6] + refs[6 + n_prev:]
        (u_ref, qnw_ref, kvnw_ref, wq_ref, wk_ref, wv_ref,
         o_ref, ckv_ref, kr_ref, q_s, k_s, v_s, ckv_s, kr_s) = refs
    L = seq_len
    S = past_len + L
    nqb = L // CHUNK
    qscale = (NOPE_DIM + ROPE_DIM) ** -0.5 * math.log2(math.e)

    q_c = _rms(u_ref[0, :, 0:Q_RANK], qnw_ref[...]).astype(BF16)
    ckv = _rms(u_ref[0, :, Q_RANK:Q_RANK + KV_RANK], kvnw_ref[...])
    kr = u_ref[0, :, Q_RANK + KV_RANK:U_MLA]
    if past_len:
        cos = cos_ref[...]
        sin = sin_ref[...]
        ckv_s[0:past_len, :] = pckv_ref[0, 0]
        kr_s[0:past_len, :] = jnp.zeros((past_len, LANES), F32)
        kr_s[0:past_len, 0:ROPE_DIM] = pkr_ref[0, 0]
        kr_s[past_len:S, :] = kr * cos + _dot_exact_lhs(kr, _rot_matrix()) * sin
    else:
        ckv_ref[0] = ckv
        kr_ref[0] = kr[:, 0:ROPE_DIM]
        kr_s[...] = kr
    ckv_s[past_len:S, :] = ckv

    ckv_all = ckv_s[...].astype(BF16)
    kr_all = kr_s[...]
    half_s = _lane_half((S, LANES))
    q_all = _dot(q_c, wq_ref[...])
    k_all = _dot(ckv_all, wk_ref[...])
    v_all = _dot(ckv_all, wv_ref[...])
    if past_len:
        qr_all = _dot(q_c, wqr_ref[...])
    for h in range(H_MLA):
        slab = slice(h * LANES, (h + 1) * LANES)
        qh = q_all[:, slab]
        if past_len:
            qh = qh * cos + qr_all[:, slab] * sin
        q_s[h] = (qh * qscale).astype(BF16)
        k_s[h] = (k_all[:, slab] + kr_all).astype(BF16)
    for pr in range(H_MLA // 2):
        vp = v_all[:, pr * LANES:(pr + 1) * LANES]
        v_s[2 * pr] = jnp.where(half_s, vp, 0.0).astype(BF16)
        v_s[2 * pr + 1] = jnp.where(half_s, 0.0, vp).astype(BF16)

    def scores(t):
        qb, h = divmod(t, H_MLA)
        return _dot_nt(q_s[h, qb * CHUNK:(qb + 1) * CHUNK, :], k_s[h])

    ntask = nqb * H_MLA
    s_next = scores(0)
    first = None
    for t in range(ntask):
        qb, h = divmod(t, H_MLA)
        s = s_next
        if t + 1 < ntask:
            s_next = scores(t + 1)
        e = jnp.exp2(s - jnp.max(s, axis=-1, keepdims=True))
        inv = 1.0 / jnp.sum(e, axis=-1, keepdims=True)
        o = _dot(e.astype(BF16), v_s[h]) * inv
        if h % 2 == 0:
            first = o
        else:
            pr = h // 2
            o_ref[0, qb * CHUNK:(qb + 1) * CHUNK, pr * LANES:(pr + 1) * LANES] = first + o


def _mla(u, seq_len, cache, layer, rope, prev_cache, qnw, kvnw, wq, wqr, wk, wv):
    nb = u.shape[0] // seq_len
    u3 = u.reshape(nb, seq_len, U_MLA)
    past_len = cache[0].shape[2] if cache is not None else 0
    s_len = past_len + seq_len
    full2 = lambda b: (0, 0)
    in_specs = [pl.BlockSpec((1, seq_len, U_MLA), lambda b: (b, 0, 0))]
    args = [u3]
    if past_len:
        in_specs += [
            pl.BlockSpec((1, 1, past_len, KV_RANK), lambda b: (b, layer, 0, 0)),
            pl.BlockSpec((1, 1, past_len, ROPE_DIM), lambda b: (b, layer, 0, 0)),
            pl.BlockSpec((seq_len, LANES), full2),
            pl.BlockSpec((seq_len, LANES), full2),
        ]
        args += [cache[0], cache[1], rope[0], rope[1]]
    in_specs += [_layer_spec(layer, 1, Q_RANK), _layer_spec(layer, 1, KV_RANK),
                 _layer_spec(layer, Q_RANK, H_MLA * LANES)]
    args += [qnw, kvnw, wq]
    if past_len:
        in_specs.append(_layer_spec(layer, Q_RANK, H_MLA * LANES))
        args.append(wqr)
    in_specs += [_layer_spec(layer, KV_RANK, H_MLA * LANES), _layer_spec(layer, KV_RANK, D_MLA)]
    args += [wk, wv]
    out_specs = [pl.BlockSpec((1, seq_len, D_MLA), lambda b: (b, 0, 0))]
    out_shape = [jax.ShapeDtypeStruct((nb, seq_len, D_MLA), F32)]
    aliases = {}
    if not past_len:
        for i, width in enumerate((KV_RANK, ROPE_DIM)):
            spec, shape, prev = _state_out(nb, layer, None if prev_cache is None else prev_cache[i],
                                           seq_len, width)
            out_specs.append(spec)
            out_shape.append(shape)
            if prev:
                aliases[len(args)] = 1 + i
                in_specs.append(pl.BlockSpec(memory_space=pl.ANY))
                args += prev
    outs = pl.pallas_call(
        functools.partial(_mla_kernel, seq_len=seq_len, past_len=past_len, n_prev=len(aliases)),
        grid=(nb,),
        in_specs=in_specs,
        out_specs=out_specs,
        out_shape=out_shape,
        input_output_aliases=aliases,
        scratch_shapes=[
            pltpu.VMEM((H_MLA, seq_len, LANES), BF16),
            pltpu.VMEM((H_MLA, s_len, LANES), BF16),
            pltpu.VMEM((H_MLA, s_len, LANES), BF16),
            pltpu.VMEM((s_len, KV_RANK), F32),
            pltpu.VMEM((s_len, LANES), F32),
        ],
        compiler_params=pltpu.CompilerParams(
            dimension_semantics=("arbitrary",), vmem_limit_bytes=VMEM_LIMIT),
        name="mla",
    )(*args)
    o = outs[0].reshape(nb * seq_len, D_MLA)
    return (o,) + tuple(outs[1:])


def _out_ffn_kernel(x_ref, mssd_ref, mret_ref, mmla_ref, mod_ref, wos_ref, wor_ref, wom_ref, n2w_ref,
                    w1a_ref, w1g_ref, w2_ref, fnw_ref, o_ref, x1_s, h2_s, acc_s, *, final):
    j = pl.program_id(1)

    @pl.when(j == 0)
    def _():
        mix = (_dot(mssd_ref[...].astype(BF16), wos_ref[...])
               + _dot(mret_ref[...].astype(BF16), wor_ref[...])
               + _dot(mmla_ref[...].astype(BF16), wom_ref[...]))
        g1 = mod_ref[0, :, 2 * D_MODEL:3 * D_MODEL]
        sh2 = mod_ref[0, :, 3 * D_MODEL:4 * D_MODEL]
        sc2 = mod_ref[0, :, 4 * D_MODEL:5 * D_MODEL]
        x1 = x_ref[...] + g1 * mix
        x1_s[...] = x1
        h2_s[...] = (_rms(x1, n2w_ref[...]) * (1.0 + sc2) + sh2).astype(BF16)
        acc_s[...] = jnp.zeros(acc_s.shape, F32)

    h2 = h2_s[...]
    a = _dot(h2, w1a_ref[...])
    gt = _dot(h2, w1g_ref[...])
    acc_s[...] += _dot((_silu(a) * gt).astype(BF16), w2_ref[...])

    @pl.when(j == pl.num_programs(1) - 1)
    def _():
        g2 = mod_ref[0, :, 5 * D_MODEL:6 * D_MODEL]
        out = x1_s[...] + g2 * acc_s[...]
        if final:
            out = _rms(out, fnw_ref[...])
        o_ref[...] = out


def _out_ffn(x, mssd, mret, mmla, mod, seq_len, shared_cond, layer, wo, n2w, w1, w2, fnw, final):
    t = x.shape[0]
    tm = _token_tile(seq_len, shared_cond)
    nj = D_FF // FF_CHUNK
    tok = lambda i, j: (i, 0)
    ret_blk = D_SSD // D_RET
    return pl.pallas_call(
        functools.partial(_out_ffn_kernel, final=final),
        grid=(t // tm, nj),
        in_specs=[
            pl.BlockSpec((tm, D_MODEL), tok),
            pl.BlockSpec((tm, D_SSD), tok),
            pl.BlockSpec((tm, D_RET), tok),
            pl.BlockSpec((tm, D_MLA), tok),
            _mod_spec(layer, None if shared_cond else seq_len // tm),
            pl.BlockSpec((None, D_SSD, D_MODEL), lambda i, j: (layer, 0, 0)),
            pl.BlockSpec((None, D_RET, D_MODEL), lambda i, j: (layer, ret_blk, 0)),
            pl.BlockSpec((None, D_MLA, D_MODEL), lambda i, j: (layer, ret_blk + 1, 0)),
            _layer_spec(layer, 1, D_MODEL),
            pl.BlockSpec((None, D_MODEL, FF_CHUNK), lambda i, j: (layer, 0, j)),
            pl.BlockSpec((None, D_MODEL, FF_CHUNK), lambda i, j: (layer, 0, j + nj)),
            pl.BlockSpec((None, FF_CHUNK, D_MODEL), lambda i, j: (layer, j, 0)),
            pl.BlockSpec((1, D_MODEL), lambda i, j: (0, 0)),
        ],
        out_specs=pl.BlockSpec((tm, D_MODEL), tok),
        out_shape=jax.ShapeDtypeStruct((t, D_MODEL), F32),
        scratch_shapes=[
            pltpu.VMEM((tm, D_MODEL), F32),
            pltpu.VMEM((tm, D_MODEL), BF16),
            pltpu.VMEM((tm, D_MODEL), F32),
        ],
        compiler_params=pltpu.CompilerParams(
            dimension_semantics=("arbitrary", "arbitrary"), vmem_limit_bytes=VMEM_LIMIT),
        name="out_ffn",
    )(x, mssd, mret, mmla, mod, wo, wo, wo, n2w, w1, w1, w2, fnw)


def _pad_lanes(a, width):
    return jnp.pad(a, [(0, 0)] * (a.ndim - 1) + [(0, width - a.shape[-1])])


def _rope_tables(n_tokens):
    n_rows = n_tokens // GRID_W
    row, col = jnp.meshgrid(jnp.arange(n_rows), jnp.arange(GRID_W), indexing='ij')
    row = row.reshape(-1).astype(F32)
    col = col.reshape(-1).astype(F32)
    half = ROPE_DIM // 2
    inv = ROPE_BASE ** (-jnp.arange(0, half, 2, dtype=F32) / half)
    ang_r = row[:, None] * inv
    ang_c = col[:, None] * inv
    ang = jnp.concatenate([ang_r, ang_r, ang_c, ang_c], axis=-1)
    cos = jnp.concatenate([jnp.cos(ang), jnp.ones((n_tokens, LANES - ROPE_DIM), F32)], axis=-1)
    sin = _pad_lanes(jnp.sin(ang), LANES)
    return cos, sin


def kernel(x_prompt, x_sample, c, state_ssd, state_ret, cache_mla_ckv, cache_mla_krope, c_ctx,
           w_ada, b_ada, norm1_w, w_in, ssd_conv_w, ssd_conv_b, ssd_dt_bias, ssd_A_log, ssd_D, ssd_norm_w,
           ret_decay_logit, ret_gn_w, mla_q_norm_w, mla_w_uq, mla_kv_norm_w, mla_w_ukv, w_out, norm2_w,
           ffn_w1, ffn_w2, final_norm_w):
    batch, seq, _ = x_prompt.shape
    dec_batch, dec_seq, _ = x_sample.shape

    ret0 = SSD_PROJ
    mla0 = SSD_PROJ + RET_PROJ
    w_ssd = _pad_lanes(w_in[..., 0:SSD_PROJ], U_SSD).astype(BF16)
    w_ret = w_in[..., ret0:mla0].astype(BF16)
    w_mla = _pad_lanes(w_in[..., mla0:], U_MLA).astype(BF16)
    uq = mla_w_uq.reshape(DEPTH, Q_RANK, H_MLA, NOPE_DIM + ROPE_DIM).transpose(0, 2, 1, 3)
    def heads_side_by_side(w):
        return w.transpose(0, 2, 1, 3).reshape(DEPTH, w.shape[2], w.shape[1] * LANES).astype(BF16)

    wq = heads_side_by_side(_pad_lanes(jnp.concatenate([uq[..., NOPE_DIM:], uq[..., :NOPE_DIM]], axis=-1), LANES))
    qr = uq[..., NOPE_DIM:].reshape(DEPTH, H_MLA, Q_RANK, 2, 2, ROPE_DIM // 4)
    wq_rot = jnp.stack([-qr[..., 1, :], qr[..., 0, :]], axis=-2).reshape(DEPTH, H_MLA, Q_RANK, ROPE_DIM)
    wq_rot = heads_side_by_side(_pad_lanes(wq_rot, LANES))
    ukv = mla_w_ukv.reshape(DEPTH, KV_RANK, H_MLA, NOPE_DIM + V_DIM).transpose(0, 2, 1, 3)
    wk = heads_side_by_side(
        jnp.pad(ukv[..., :NOPE_DIM], [(0, 0)] * 3 + [(ROPE_DIM, LANES - ROPE_DIM - NOPE_DIM)]))
    wv = ukv[..., NOPE_DIM:].transpose(0, 2, 1, 3).reshape(DEPTH, KV_RANK, D_MLA).astype(BF16)
    wo = w_out.astype(BF16)
    w1 = ffn_w1.astype(BF16)
    w2 = ffn_w2.astype(BF16)
    conv_w = jnp.pad(ssd_conv_w, [(0, 0), (0, 8 - CONV_W), (0, 0)])
    conv_b = ssd_conv_b.reshape(DEPTH, 1, CONV_DIM)
    dt_bias = _pad_lanes(ssd_dt_bias.reshape(DEPTH, 1, 2 * H_SSD), LANES)
    a_log = _pad_lanes(ssd_A_log.reshape(DEPTH, 1, 2 * H_SSD), LANES)
    d_skip = jnp.repeat(ssd_D, P_SSD, axis=-1).reshape(DEPTH, 1, D_SSD)
    ssd_nw = ssd_norm_w.reshape(DEPTH, 1, D_SSD)
    decay_logit = _pad_lanes(ret_decay_logit.reshape(DEPTH, 1, 2 * H_RET), LANES)
    gn_w = ret_gn_w.reshape(DEPTH, 1, D_RET)
    q_nw = mla_q_norm_w.reshape(DEPTH, 1, Q_RANK)
    kv_nw = mla_kv_norm_w.reshape(DEPTH, 1, KV_RANK)
    n1w = norm1_w.reshape(DEPTH, 1, D_MODEL)
    n2w = norm2_w.reshape(DEPTH, 1, D_MODEL)
    fnw = final_norm_w.reshape(1, D_MODEL)
    rope = _rope_tables(dec_seq)

    assert dec_batch < N_COND
    cond = jnp.concatenate([c_ctx[None, :], c, jnp.zeros((N_COND - 1 - dec_batch, D_MODEL), F32)], axis=0)
    mod = _modulation(cond, w_ada, b_ada).reshape(DEPTH * N_COND, 1, 6 * D_MODEL)

    def layer(i, x, seq_len, shared_cond, h0_ssd, h0_ret, cache, prev):
        final = i == DEPTH - 1
        u_ssd, u_ret, u_mla = _proj_in(x, mod, seq_len, shared_cond, i, n1w, w_ssd, w_ret, w_mla)
        y_ssd, s_ssd = _ssd(u_ssd, seq_len, h0_ssd, i, prev[0], conv_w, conv_b, dt_bias, a_log, d_skip, ssd_nw)
        y_ret, s_ret = _ret(u_ret, seq_len, h0_ret, i, prev[1], decay_logit, gn_w)
        mla_out = _mla(u_mla, seq_len, cache, i, rope, prev[2], q_nw, kv_nw, wq, wq_rot, wk, wv)
        x = _out_ffn(x, y_ssd, y_ret, mla_out[0], mod, seq_len, shared_cond, i, wo, n2w, w1, w2, fnw, final)
        return x, (s_ssd, s_ret, mla_out[1:])

    xp = x_prompt.reshape(batch * seq, D_MODEL)
    stacked = (None, None, None)
    for i in range(DEPTH):
        xp, stacked = layer(i, xp, seq, True, None, None, None, stacked)

    xs = x_sample.reshape(dec_batch * dec_seq, D_MODEL)
    for i in range(DEPTH):
        xs, _ = layer(i, xs, dec_seq, False, state_ssd, state_ret, (cache_mla_ckv, cache_mla_krope),
                      (None, None, None))

    new_ssd, new_ret, (new_ckv, new_kr) = stacked
    return (xp.reshape(batch, seq, D_MODEL), xs.reshape(dec_batch, dec_seq, D_MODEL),
            new_ssd, new_ret, new_ckv, new_kr)
```

```python
import functools
import math

import jax
import jax.numpy as jnp
from jax import lax
from jax.experimental import pallas as pl
from jax.experimental.pallas import tpu as pltpu

F32 = jnp.float32
BF16 = jnp.bfloat16

D_MODEL = 1024
DEPTH = 4
GRID_W = 64
CHUNK = 128
LANES = 128
SUBLANES = 8
HALO = CHUNK // 2
EPS = 1e-6

H_SSD, P_SSD, N_SSD, G_SSD = 8, 64, 64, 2
D_SSD = H_SSD * P_SSD
CONV_W = 5
CONV_DIM = D_SSD + 2 * G_SSD * N_SSD
SSD_PROJ = D_SSD + CONV_DIM + 2 * H_SSD
U_SSD = D_SSD + CONV_DIM + LANES

H_RET, DK_RET, DV_RET = 4, 64, 64
D_RET = H_RET * DV_RET
RET_PROJ = 2 * H_RET * DK_RET + 2 * D_RET

H_MLA, Q_RANK, KV_RANK, NOPE_DIM, ROPE_DIM, V_DIM = 4, 256, 128, 64, 32, 64
D_MLA = H_MLA * V_DIM
U_MLA = Q_RANK + KV_RANK + LANES
ROPE_BASE = 10000.0

D_FF = 2816
FF_SPLITS = ((0, 1024), (1024, 2048), (2048, D_FF))
TOKEN_TILE = 512
PROJ_ROWS = 256
N_COND = 16
VMEM_LIMIT = 56 * 1024 * 1024


def _dot(a, b):
    return jnp.dot(a, b, preferred_element_type=F32)


def _dot_nt(a, b):
    return lax.dot_general(a, b, (((1,), (1,)), ((), ())), preferred_element_type=F32)


def _split3(x):
    hi = x.astype(BF16)
    r1 = x - hi.astype(F32)
    mid = r1.astype(BF16)
    lo = (r1 - mid.astype(F32)).astype(BF16)
    return hi, mid, lo


def _dot_exact_rhs(ones_bf16, x):
    hi, mid, lo = _split3(x)
    return _dot(ones_bf16, hi) + _dot(ones_bf16, mid) + _dot(ones_bf16, lo)


def _dot_exact_lhs(x, ones_bf16):
    hi, mid, lo = _split3(x)
    return _dot(hi, ones_bf16) + _dot(mid, ones_bf16) + _dot(lo, ones_bf16)


def _rms(x, w):
    return x * lax.rsqrt(jnp.mean(x * x, axis=-1, keepdims=True) + EPS) * w


def _silu(x):
    h = 0.5 * x
    return h + h * jnp.tanh(h)


def _softplus(x):
    return jnp.maximum(x, 0.0) + jnp.log1p(jnp.exp(-jnp.abs(x)))


def _lane_half(shape):
    return lax.broadcasted_iota(jnp.int32, shape, len(shape) - 1) % LANES < (LANES // 2)


def _mod_kernel(c_ref, w_ref, b_ref, o_ref):
    s = _silu(c_ref[...]).astype(BF16)
    o_ref[0] = _dot(s, w_ref[0].astype(BF16)) + b_ref[0]


def _modulation(cond, w_ada, b_ada):
    rows = cond.shape[0]
    n = w_ada.shape[-1]
    tn = 1536
    return pl.pallas_call(
        _mod_kernel,
        grid=(DEPTH, n // tn),
        in_specs=[
            pl.BlockSpec((rows, D_MODEL), lambda l, j: (0, 0)),
            pl.BlockSpec((1, D_MODEL, tn), lambda l, j: (l, 0, j)),
            pl.BlockSpec((1, 1, tn), lambda l, j: (l, 0, j)),
        ],
        out_specs=pl.BlockSpec((1, rows, tn), lambda l, j: (l, 0, j)),
        out_shape=jax.ShapeDtypeStruct((DEPTH, rows, n), F32),
        compiler_params=pltpu.CompilerParams(
            dimension_semantics=("arbitrary", "arbitrary"), vmem_limit_bytes=VMEM_LIMIT),
        name="modulation",
    )(cond, w_ada, b_ada.reshape(DEPTH, 1, n))


def _mod_spec(layer, per_row_tiles):
    base = layer * N_COND
    if per_row_tiles is None:
        index = lambda i, *_: (base, 0, 0)
    else:
        index = lambda i, *_: (base + 1 + i // per_row_tiles, 0, 0)
    return pl.BlockSpec((1, 1, 6 * D_MODEL), index)


def _token_tile(seq_len, shared_cond):
    return TOKEN_TILE if shared_cond else min(TOKEN_TILE, seq_len)


def _project_in(x_ref, mod_ref, n1w_ref, w_ref, u_s):
    sh1 = mod_ref[0, :, 0:D_MODEL]
    sc1 = mod_ref[0, :, D_MODEL:2 * D_MODEL]
    for r in range(0, x_ref.shape[1], PROJ_ROWS):
        h = (_rms(x_ref[0, r:r + PROJ_ROWS, :], n1w_ref[...]) * (1.0 + sc1) + sh1).astype(BF16)
        u_s[r:r + PROJ_ROWS, :] = _dot(h, w_ref[...])


def _ssd_kernel(*refs, seq_len, has_h0, has_prev):
    if has_h0:
        (x_ref, mod_ref, n1w_ref, w_ref, h0_ref, cw_ref, cb_ref, dtb_ref, alog_ref, dsk_ref, nw_ref,
         y_ref, u_s, xpad_s, xbc_s, dt_s, la_s, y_s, st_s) = refs
    else:
        refs = refs[:10] + refs[10 + has_prev:]
        (x_ref, mod_ref, n1w_ref, w_ref, cw_ref, cb_ref, dtb_ref, alog_ref, dsk_ref, nw_ref,
         y_ref, hout_ref, u_s, xpad_s, xbc_s, dt_s, la_s, y_s, st_s) = refs
    L = seq_len
    nc = L // CHUNK
    _project_in(x_ref, mod_ref, n1w_ref, w_ref, u_s)

    xpad_s[0:HALO, :] = jnp.zeros((HALO, CONV_DIM), BF16)
    xpad_s[HALO + L:2 * HALO + L, :] = jnp.zeros((HALO, CONV_DIM), BF16)
    xpad_s[HALO:HALO + L, :] = u_s[:, D_SSD:D_SSD + CONV_DIM].astype(BF16)
    side_taps = [k for k in range(CONV_W) if k != CONV_W // 2]
    sr = lax.broadcasted_iota(jnp.int32, (len(side_taps) * CHUNK, 2 * CHUNK), 0)
    sc = lax.broadcasted_iota(jnp.int32, (len(side_taps) * CHUNK, 2 * CHUNK), 1)
    src_row = sr % CHUNK + HALO
    for i, k in enumerate(side_taps):
        src_row = src_row + jnp.where(sr // CHUNK == i, k - CONV_W // 2, 0)
    shift = jnp.where(sc == src_row, 1.0, 0.0).astype(BF16)

    for c in range(nc):
        base = c * CHUNK
        moved = _dot(shift, xpad_s[base:base + 2 * CHUNK, :])
        centre = u_s[base:base + CHUNK, D_SSD:D_SSD + CONV_DIM]
        acc = cb_ref[...] + centre * cw_ref[CONV_W // 2:CONV_W // 2 + 1, :]
        for i, k in enumerate(side_taps):
            acc = acc + moved[i * CHUNK:(i + 1) * CHUNK, :] * cw_ref[k:k + 1, :]
        xbc_s[base:base + CHUNK, :] = _silu(acc)

    dt = _softplus(u_s[:, D_SSD + CONV_DIM:U_SSD] + dtb_ref[...])
    dt_s[...] = dt
    la_s[...] = dt * (-jnp.exp(alog_ref[...]))
    y_s[...] = xbc_s[:, 0:D_SSD] * dsk_ref[...]

    if has_h0:
        for d in range(2):
            for h in range(H_SSD):
                g, hh = divmod(h, H_SSD // G_SSD)
                st_s[d, g * N_SSD:(g + 1) * N_SSD, hh * P_SSD:(hh + 1) * P_SSD] = h0_ref[0, 0, d, h]
    else:
        st_s[...] = jnp.zeros(st_s.shape, F32)

    ri = lax.broadcasted_iota(jnp.int32, (CHUNK, CHUNK), 0)
    ci = lax.broadcasted_iota(jnp.int32, (CHUNK, CHUNK), 1)
    first_group_lanes = _lane_half((CHUNK, LANES))
    first_group_rows = lax.broadcasted_iota(jnp.int32, (2 * N_SSD, 4 * P_SSD), 0) < N_SSD
    half_row = _lane_half((1, LANES))
    er = lax.broadcasted_iota(jnp.int32, (LANES, D_SSD), 0)
    el = lax.broadcasted_iota(jnp.int32, (LANES, D_SSD), 1) // P_SSD
    expand = [jnp.where(er == d * H_SSD + el, 1.0, 0.0).astype(BF16) for d in range(2)]

    keeps = [ri >= ci, ri <= ci]
    tris = [jnp.where(kp, 1.0, 0.0).astype(BF16) for kp in keeps]

    def stage_a(start, d):
        rows = pl.ds(start, CHUNK)
        cs = _dot_exact_rhs(tris[d], la_s[rows, :])
        tot = cs[CHUNK - 1:CHUNK, :] if d == 0 else cs[0:1, :]
        dt_wide = _dot_exact_lhs(dt_s[rows, :], expand[d])
        tot_wide = _dot_exact_lhs(jnp.broadcast_to(tot, (SUBLANES, LANES)), expand[d])[0:1, :]
        bmat = xbc_s[rows, D_SSD:D_SSD + LANES]
        cmat = xbc_s[rows, D_SSD + LANES:D_SSD + 2 * LANES]
        b_bf = bmat.astype(BF16)
        c_grp = [jnp.where(first_group_lanes, cmat, 0.0).astype(BF16),
                 jnp.where(first_group_lanes, 0.0, cmat).astype(BF16)]
        gmat = [_dot_nt(c_grp[g], b_bf) for g in range(G_SSD)]
        st_bf = st_s[d].astype(BF16)
        y_inter = [_dot(c_grp[g], st_bf) for g in range(G_SSD)]
        return dict(rows=rows, d=d, cs=cs, cst=cs.T, tot=tot, dt_wide=dt_wide, tot_wide=tot_wide,
                    gmat=gmat, y_inter=y_inter, bt=bmat.T.astype(BF16))

    def stage_b(t):
        rows, d, cs, cst, tot, dt_wide = t["rows"], t["d"], t["cs"], t["cst"], t["tot"], t["dt_wide"]
        gmat, y_inter, keep = t["gmat"], t["y_inter"], keeps[t["d"]]
        vte = []
        for pr in range(H_SSD // 2):
            g = pr // 2
            ia = d * H_SSD + 2 * pr
            ib = ia + 1
            pair = slice(pr * LANES, (pr + 1) * LANES)
            v = xbc_s[rows, pair] * dt_wide[:, pair]
            va = jnp.where(first_group_lanes, v, 0.0).astype(BF16)
            vb = jnp.where(first_group_lanes, 0.0, v).astype(BF16)
            col_a = jnp.broadcast_to(cs[:, ia:ia + 1], (CHUNK, LANES))
            col_b = jnp.broadcast_to(cs[:, ib:ib + 1], (CHUNK, LANES))
            seg_a = col_a - cst[ia:ia + 1, :]
            seg_b = col_b - cst[ib:ib + 1, :]
            s_a = (gmat[g] * jnp.exp(jnp.where(keep, seg_a, -jnp.inf))).astype(BF16)
            s_b = (gmat[g] * jnp.exp(jnp.where(keep, seg_b, -jnp.inf))).astype(BF16)
            y_intra = _dot(s_a, va) + _dot(s_b, vb)
            col_pair = jnp.where(first_group_lanes, col_a, col_b)
            tot_pair = jnp.where(half_row, tot[:, ia:ia + 1], tot[:, ib:ib + 1])
            lo = (pr % 2) * LANES
            y_pair = y_intra + y_inter[g][:, lo:lo + LANES] * jnp.exp(col_pair)
            y_s[rows, pair] += y_pair
            vte.append((v * jnp.exp(tot_pair - col_pair)).astype(BF16))

        bt = t["bt"]
        new0 = _dot(bt, jnp.concatenate([vte[0], vte[1]], axis=1))
        new1 = _dot(bt, jnp.concatenate([vte[2], vte[3]], axis=1))
        new = jnp.where(first_group_rows, new0, new1)
        cd = jnp.exp(t["tot_wide"])
        half_w = 4 * P_SSD
        decay = jnp.where(first_group_rows, cd[:, 0:half_w], cd[:, half_w:2 * half_w])
        st_s[d] = st_s[d] * decay + new

    tasks = [((c if d == 0 else nc - 1 - c) * CHUNK, d) for c in range(nc) for d in range(2)]
    pending = stage_a(*tasks[0])
    for nxt in tasks[1:]:
        upcoming = stage_a(*nxt)
        stage_b(pending)
        pending = upcoming
    stage_b(pending)

    if not has_h0:
        for d in range(2):
            for h in range(H_SSD):
                g, hh = divmod(h, H_SSD // G_SSD)
                hout_ref[0, d, h] = st_s[d, g * N_SSD:(g + 1) * N_SSD, hh * P_SSD:(hh + 1) * P_SSD]

    y = y_s[...] * _silu(u_s[:, 0:D_SSD])
    y_ref[0] = _rms(y, nw_ref[...])


def _layer_spec(layer, *tail):
    zeros = (0,) * len(tail)
    return pl.BlockSpec((None,) + tail, lambda *_: (layer,) + zeros)


def _state_out(nb, layer, prev, *tail):
    zeros = (0,) * len(tail)
    spec = pl.BlockSpec((1, None) + tail, lambda b: (b, layer) + zeros)
    shape = jax.ShapeDtypeStruct((nb, DEPTH) + tail, F32)
    return spec, shape, ([] if prev is None else [prev])


def _seq_inputs(x, mod, seq_len, shared_cond, layer, n1w, w, width):
    nb = x.shape[0] // seq_len
    specs = [pl.BlockSpec((1, seq_len, D_MODEL), lambda b: (b, 0, 0)),
             _mod_spec(layer, None if shared_cond else 1),
             _layer_spec(layer, 1, D_MODEL),
             _layer_spec(layer, D_MODEL, width)]
    return nb, specs, [x.reshape(nb, seq_len, D_MODEL), mod, n1w, w]


def _ssd(x, mod, seq_len, shared_cond, h0, layer, prev_states, n1w, w, cw, cb, dtb, alog, dsk, nw):
    nb, in_specs, args = _seq_inputs(x, mod, seq_len, shared_cond, layer, n1w, w, U_SSD)
    has_h0 = h0 is not None
    if has_h0:
        in_specs.append(pl.BlockSpec((1, 1, 2, H_SSD, N_SSD, P_SSD), lambda b: (b, layer, 0, 0, 0, 0)))
        args.append(h0)
    in_specs += [
        _layer_spec(layer, 8, CONV_DIM),
        _layer_spec(layer, 1, CONV_DIM),
        _layer_spec(layer, 1, LANES),
        _layer_spec(layer, 1, LANES),
        _layer_spec(layer, 1, D_SSD),
        _layer_spec(layer, 1, D_SSD),
    ]
    args += [cw, cb, dtb, alog, dsk, nw]
    out_specs = [pl.BlockSpec((1, seq_len, D_SSD), lambda b: (b, 0, 0))]
    out_shape = [jax.ShapeDtypeStruct((nb, seq_len, D_SSD), F32)]
    aliases = {}
    if not has_h0:
        spec, shape, prev = _state_out(nb, layer, prev_states, 2, H_SSD, N_SSD, P_SSD)
        out_specs.append(spec)
        out_shape.append(shape)
        if prev:
            aliases = {len(args): 1}
            in_specs.append(pl.BlockSpec(memory_space=pl.ANY))
            args += prev
    outs = pl.pallas_call(
        functools.partial(_ssd_kernel, seq_len=seq_len, has_h0=has_h0, has_prev=bool(aliases)),
        grid=(nb,),
        in_specs=in_specs,
        out_specs=out_specs,
        out_shape=out_shape,
        input_output_aliases=aliases,
        scratch_shapes=[
            pltpu.VMEM((seq_len, U_SSD), F32),
            pltpu.VMEM((seq_len + 2 * HALO, CONV_DIM), BF16),
            pltpu.VMEM((seq_len, CONV_DIM), F32),
            pltpu.VMEM((seq_len, LANES), F32),
            pltpu.VMEM((seq_len, LANES), F32),
            pltpu.VMEM((seq_len, D_SSD), F32),
            pltpu.VMEM((2, 2 * N_SSD, 4 * P_SSD), F32),
        ],
        compiler_params=pltpu.CompilerParams(
            dimension_semantics=("arbitrary",), vmem_limit_bytes=VMEM_LIMIT),
        name="ssd",
    )(*args)
    return outs[0].reshape(nb * seq_len, D_SSD), (None if has_h0 else outs[1])


def _ret_kernel(*refs, seq_len, has_h0, has_prev):
    if has_h0:
        x_ref, mod_ref, n1w_ref, w_ref, h0_ref, dl_ref, gnw_ref, y_ref, u_s, y_s, st_s = refs
    else:
        refs = refs[:6] + refs[6 + has_prev:]
        x_ref, mod_ref, n1w_ref, w_ref, dl_ref, gnw_ref, y_ref, hout_ref, u_s, y_s, st_s = refs
    _project_in(x_ref, mod_ref, n1w_ref, w_ref, u_s)
    L = seq_len
    nc = L // CHUNK
    npair = H_RET // 2
    nq = H_RET * DK_RET

    lg_row = -_softplus(-dl_ref[...])
    ri = lax.broadcasted_iota(jnp.int32, (CHUNK, CHUNK), 0)
    ci = lax.broadcasted_iota(jnp.int32, (CHUNK, CHUNK), 1)
    rif = ri.astype(F32)
    half = _lane_half((CHUNK, LANES))
    block_diag = (ri < DK_RET) == (ci < DV_RET)

    if has_h0:
        st_s[...] = jnp.zeros(st_s.shape, F32)
        for d in range(2):
            for h in range(H_RET):
                pr, hh = divmod(h, 2)
                st_s[d, pr, hh * DK_RET:(hh + 1) * DK_RET, hh * DV_RET:(hh + 1) * DV_RET] = h0_ref[0, 0, d, h]
    else:
        st_s[...] = jnp.zeros(st_s.shape, F32)
    y_s[...] = jnp.zeros(y_s.shape, F32)

    consts = {}
    for d in range(2):
        keep = (ri >= ci) if d == 0 else (ri <= ci)
        dist = (ri - ci).astype(F32) if d == 0 else (ci - ri).astype(F32)
        for pr in range(npair):
            ia = d * H_RET + 2 * pr
            lg_a = lg_row[:, ia:ia + 1]
            lg_b = lg_row[:, ia + 1:ia + 2]
            lg_pair = jnp.where(half, lg_a, lg_b)
            if d == 0:
                in_scale = jnp.exp((rif + 1.0) * lg_pair)
                to_end = jnp.exp((CHUNK - 1.0 - rif) * lg_pair)
            else:
                in_scale = jnp.exp((CHUNK - rif) * lg_pair)
                to_end = jnp.exp(rif * lg_pair)
            consts[d, pr] = dict(
                dec_a=jnp.exp(jnp.where(keep, dist * lg_a, -jnp.inf)),
                dec_b=jnp.exp(jnp.where(keep, dist * lg_b, -jnp.inf)),
                in_scale=in_scale, to_end=to_end, cd=jnp.exp(float(CHUNK) * lg_pair))

    def stage_a(start, d, pr):
        cst = consts[d, pr]
        rows = pl.ds(start, CHUNK)
        lanes = slice(pr * LANES, (pr + 1) * LANES)
        q = u_s[rows, lanes]
        k = u_s[rows, nq + pr * LANES:nq + (pr + 1) * LANES] * (DK_RET ** -0.5)
        v = u_s[rows, 2 * nq + pr * LANES:2 * nq + (pr + 1) * LANES]
        k_bf = k.astype(BF16)
        qk_a = _dot_nt(jnp.where(half, q, 0.0).astype(BF16), k_bf)
        qk_b = _dot_nt(jnp.where(half, 0.0, q).astype(BF16), k_bf)
        st = st_s[d, pr]
        y_inter = _dot(q.astype(BF16), st.astype(BF16)) * cst["in_scale"]
        new = _dot(k.T.astype(BF16), (v * cst["to_end"]).astype(BF16))
        st_s[d, pr] = st * cst["cd"] + jnp.where(block_diag, new, 0.0)
        return dict(rows=rows, lanes=lanes, cst=cst, qk_a=qk_a, qk_b=qk_b, y_inter=y_inter,
                    va=jnp.where(half, v, 0.0).astype(BF16), vb=jnp.where(half, 0.0, v).astype(BF16))

    def stage_b(t):
        s_a = (t["qk_a"] * t["cst"]["dec_a"]).astype(BF16)
        s_b = (t["qk_b"] * t["cst"]["dec_b"]).astype(BF16)
        y_s[t["rows"], t["lanes"]] += _dot(s_a, t["va"]) + _dot(s_b, t["vb"]) + t["y_inter"]

    tasks = [((c if d == 0 else nc - 1 - c) * CHUNK, d, pr)
             for c in range(nc) for d in range(2) for pr in range(npair)]
    pending = stage_a(*tasks[0])
    for nxt in tasks[1:]:
        upcoming = stage_a(*nxt)
        stage_b(pending)
        pending = upcoming
    stage_b(pending)

    if not has_h0:
        for d in range(2):
            for h in range(H_RET):
                pr, hh = divmod(h, 2)
                hout_ref[0, d, h] = st_s[d, pr, hh * DK_RET:(hh + 1) * DK_RET, hh * DV_RET:(hh + 1) * DV_RET]

    half_l = _lane_half((L, LANES))
    for pr in range(npair):
        lanes = slice(pr * LANES, (pr + 1) * LANES)
        o = y_s[:, lanes]
        inv = 1.0 / DV_RET
        sum_a = jnp.sum(jnp.where(half_l, o, 0.0), axis=-1, keepdims=True)
        sum_all = jnp.sum(o, axis=-1, keepdims=True)
        mu = jnp.where(half_l, sum_a, sum_all - sum_a) * inv
        c = o - mu
        c2 = c * c
        sq_a = jnp.sum(jnp.where(half_l, c2, 0.0), axis=-1, keepdims=True)
        sq_all = jnp.sum(c2, axis=-1, keepdims=True)
        var = jnp.where(half_l, sq_a, sq_all - sq_a) * inv
        on = c * lax.rsqrt(var + EPS) * gnw_ref[:, lanes]
        gate = u_s[:, 2 * nq + D_RET + pr * LANES:2 * nq + D_RET + (pr + 1) * LANES]
        y_ref[0, :, lanes] = _silu(gate) * on


def _ret(x, mod, seq_len, shared_cond, h0, layer, prev_states, n1w, w, dl, gnw):
    nb, in_specs, args = _seq_inputs(x, mod, seq_len, shared_cond, layer, n1w, w, RET_PROJ)
    has_h0 = h0 is not None
    if has_h0:
        in_specs.append(pl.BlockSpec((1, 1, 2, H_RET, DK_RET, DV_RET), lambda b: (b, layer, 0, 0, 0, 0)))
        args.append(h0)
    in_specs += [_layer_spec(layer, 1, LANES), _layer_spec(layer, 1, D_RET)]
    args += [dl, gnw]
    out_specs = [pl.BlockSpec((1, seq_len, D_RET), lambda b: (b, 0, 0))]
    out_shape = [jax.ShapeDtypeStruct((nb, seq_len, D_RET), F32)]
    aliases = {}
    if not has_h0:
        spec, shape, prev = _state_out(nb, layer, prev_states, 2, H_RET, DK_RET, DV_RET)
        out_specs.append(spec)
        out_shape.append(shape)
        if prev:
            aliases = {len(args): 1}
            in_specs.append(pl.BlockSpec(memory_space=pl.ANY))
            args += prev
    outs = pl.pallas_call(
        functools.partial(_ret_kernel, seq_len=seq_len, has_h0=has_h0, has_prev=bool(aliases)),
        grid=(nb,),
        in_specs=in_specs,
        out_specs=out_specs,
        out_shape=out_shape,
        input_output_aliases=aliases,
        scratch_shapes=[
            pltpu.VMEM((seq_len, RET_PROJ), F32),
            pltpu.VMEM((seq_len, D_RET), F32),
            pltpu.VMEM((2, H_RET // 2, 2 * DK_RET, 2 * DV_RET), F32),
        ],
        compiler_params=pltpu.CompilerParams(
            dimension_semantics=("arbitrary",), vmem_limit_bytes=VMEM_LIMIT),
        name="retention",
    )(*args)
    return outs[0].reshape(nb * seq_len, D_RET), (None if has_h0 else outs[1])


def _rot_matrix():
    r = lax.broadcasted_iota(jnp.int32, (LANES, LANES), 0)
    l = lax.broadcasted_iota(jnp.int32, (LANES, LANES), 1)
    even = (l // 8) % 2 == 0
    rope = l < ROPE_DIM
    p = jnp.where(rope & even & (r == l + 8), -1.0, 0.0) + jnp.where(rope & ~even & (r == l - 8), 1.0, 0.0)
    return p.astype(BF16)


def _mla_kernel(*refs, seq_len, past_len, n_prev):
    if past_len:
        (x_ref, mod_ref, n1w_ref, w_ref, pckv_ref, pkr_ref, cos_ref, sin_ref, qnw_ref, kvnw_ref,
         wq_ref, wqr_ref, wk_ref, wv_ref, o_ref, u_s, q_s, k_s, v_s, ckv_s, kr_s) = refs
    else:
        refs = refs[:9] + refs[9 + n_prev:]
        (x_ref, mod_ref, n1w_ref, w_ref, qnw_ref, kvnw_ref, wq_ref, wk_ref, wv_ref,
         o_ref, ckv_ref, kr_ref, u_s, q_s, k_s, v_s, ckv_s, kr_s) = refs
    _project_in(x_ref, mod_ref, n1w_ref, w_ref, u_s)
    L = seq_len
    S = past_len + L
    nqb = L // CHUNK
    qscale = (NOPE_DIM + ROPE_DIM) ** -0.5 * math.log2(math.e)

    q_c = _rms(u_s[:, 0:Q_RANK], qnw_ref[...]).astype(BF16)
    ckv = _rms(u_s[:, Q_RANK:Q_RANK + KV_RANK], kvnw_ref[...])
    kr = u_s[:, Q_RANK + KV_RANK:U_MLA]
    if past_len:
        cos = cos_ref[...]
        sin = sin_ref[...]
        ckv_s[0:past_len, :] = pckv_ref[0, 0]
        kr_s[0:past_len, :] = jnp.zeros((past_len, LANES), F32)
        kr_s[0:past_len, 0:ROPE_DIM] = pkr_ref[0, 0]
        kr_s[past_len:S, :] = kr * cos + _dot_exact_lhs(kr, _rot_matrix()) * sin
    else:
        ckv_ref[0] = ckv
        kr_ref[0] = kr[:, 0:ROPE_DIM]
        kr_s[...] = kr
    ckv_s[past_len:S, :] = ckv

    ckv_all = ckv_s[...].astype(BF16)
    kr_all = kr_s[...]
    half_s = _lane_half((S, LANES))
    q_all = _dot(q_c, wq_ref[...])
    k_all = _dot(ckv_all, wk_ref[...])
    v_all = _dot(ckv_all, wv_ref[...])
    if past_len:
        qr_all = _dot(q_c, wqr_ref[...])
    for h in range(H_MLA):
        slab = slice(h * LANES, (h + 1) * LANES)
        qh = q_all[:, slab]
        if past_len:
            qh = qh * cos + qr_all[:, slab] * sin
        q_s[h] = (qh * qscale).astype(BF16)
        k_s[h] = (k_all[:, slab] + kr_all).astype(BF16)
    for pr in range(H_MLA // 2):
        vp = v_all[:, pr * LANES:(pr + 1) * LANES]
        v_s[2 * pr] = jnp.where(half_s, vp, 0.0).astype(BF16)
        v_s[2 * pr + 1] = jnp.where(half_s, 0.0, vp).astype(BF16)

    def scores(t):
        qb, h = divmod(t, H_MLA)
        return _dot_nt(q_s[h, qb * CHUNK:(qb + 1) * CHUNK, :], k_s[h])

    ntask = nqb * H_MLA
    s_next = scores(0)
    first = None
    for t in range(ntask):
        qb, h = divmod(t, H_MLA)
        s = s_next
        if t + 1 < ntask:
            s_next = scores(t + 1)
        e = jnp.exp2(s - jnp.max(s, axis=-1, keepdims=True))
        inv = 1.0 / jnp.sum(e, axis=-1, keepdims=True)
        o = _dot(e.astype(BF16), v_s[h]) * inv
        if h % 2 == 0:
            first = o
        else:
            pr = h // 2
            o_ref[0, qb * CHUNK:(qb + 1) * CHUNK, pr * LANES:(pr + 1) * LANES] = first + o


def _mla(x, mod, seq_len, shared_cond, cache, layer, rope, prev_cache, n1w, w, qnw, kvnw, wq, wqr, wk, wv):
    nb, in_specs, args = _seq_inputs(x, mod, seq_len, shared_cond, layer, n1w, w, U_MLA)
    past_len = cache[0].shape[2] if cache is not None else 0
    s_len = past_len + seq_len
    full2 = lambda b: (0, 0)
    if past_len:
        in_specs += [
            pl.BlockSpec((1, 1, past_len, KV_RANK), lambda b: (b, layer, 0, 0)),
            pl.BlockSpec((1, 1, past_len, ROPE_DIM), lambda b: (b, layer, 0, 0)),
            pl.BlockSpec((seq_len, LANES), full2),
            pl.BlockSpec((seq_len, LANES), full2),
        ]
        args += [cache[0], cache[1], rope[0], rope[1]]
    in_specs += [_layer_spec(layer, 1, Q_RANK), _layer_spec(layer, 1, KV_RANK),
                 _layer_spec(layer, Q_RANK, H_MLA * LANES)]
    args += [qnw, kvnw, wq]
    if past_len:
        in_specs.append(_layer_spec(layer, Q_RANK, H_MLA * LANES))
        args.append(wqr)
    in_specs += [_layer_spec(layer, KV_RANK, H_MLA * LANES), _layer_spec(layer, KV_RANK, D_MLA)]
    args += [wk, wv]
    out_specs = [pl.BlockSpec((1, seq_len, D_MLA), lambda b: (b, 0, 0))]
    out_shape = [jax.ShapeDtypeStruct((nb, seq_len, D_MLA), F32)]
    aliases = {}
    if not past_len:
        for i, width in enumerate((KV_RANK, ROPE_DIM)):
            spec, shape, prev = _state_out(nb, layer, None if prev_cache is None else prev_cache[i],
                                           seq_len, width)
            out_specs.append(spec)
            out_shape.append(shape)
            if prev:
                aliases[len(args)] = 1 + i
                in_specs.append(pl.BlockSpec(memory_space=pl.ANY))
                args += prev
    outs = pl.pallas_call(
        functools.partial(_mla_kernel, seq_len=seq_len, past_len=past_len, n_prev=len(aliases)),
        grid=(nb,),
        in_specs=in_specs,
        out_specs=out_specs,
        out_shape=out_shape,
        input_output_aliases=aliases,
        scratch_shapes=[
            pltpu.VMEM((seq_len, U_MLA), F32),
            pltpu.VMEM((H_MLA, seq_len, LANES), BF16),
            pltpu.VMEM((H_MLA, s_len, LANES), BF16),
            pltpu.VMEM((H_MLA, s_len, LANES), BF16),
            pltpu.VMEM((s_len, KV_RANK), F32),
            pltpu.VMEM((s_len, LANES), F32),
        ],
        compiler_params=pltpu.CompilerParams(
            dimension_semantics=("arbitrary",), vmem_limit_bytes=VMEM_LIMIT),
        name="mla",
    )(*args)
    o = outs[0].reshape(nb * seq_len, D_MLA)
    return (o,) + tuple(outs[1:])


def _out_ffn_kernel(x_ref, mssd_ref, mret_ref, mmla_ref, mod_ref, wos_ref, wor_ref, wom_ref, n2w_ref,
                    w1_ref, w2_ref, fnw_ref, o_ref, *, final):
    mix = (_dot(mssd_ref[...].astype(BF16), wos_ref[...])
           + _dot(mret_ref[...].astype(BF16), wor_ref[...])
           + _dot(mmla_ref[...].astype(BF16), wom_ref[...]))
    g1 = mod_ref[0, :, 2 * D_MODEL:3 * D_MODEL]
    sh2 = mod_ref[0, :, 3 * D_MODEL:4 * D_MODEL]
    sc2 = mod_ref[0, :, 4 * D_MODEL:5 * D_MODEL]
    g2 = mod_ref[0, :, 5 * D_MODEL:6 * D_MODEL]
    x1 = x_ref[...] + g1 * mix
    h2 = (_rms(x1, n2w_ref[...]) * (1.0 + sc2) + sh2).astype(BF16)

    def up(c):
        lo, hi = FF_SPLITS[c]
        return _dot(h2, w1_ref[:, lo:hi]), _dot(h2, w1_ref[:, D_FF + lo:D_FF + hi])

    acc = None
    nxt = up(0)
    for c, (lo, hi) in enumerate(FF_SPLITS):
        a, gt = nxt
        if c + 1 < len(FF_SPLITS):
            nxt = up(c + 1)
        down = _dot((_silu(a) * gt).astype(BF16), w2_ref[lo:hi, :])
        acc = down if acc is None else acc + down
    out = x1 + g2 * acc
    if final:
        out = _rms(out, fnw_ref[...])
    o_ref[...] = out


def _out_ffn(x, mssd, mret, mmla, mod, seq_len, shared_cond, layer, wo, n2w, w1, w2, fnw, final):
    t = x.shape[0]
    tm = _token_tile(seq_len, shared_cond)
    tok = lambda i: (i, 0)
    ret_blk = D_SSD // D_RET
    once = pl.Buffered(1)
    return pl.pallas_call(
        functools.partial(_out_ffn_kernel, final=final),
        grid=(t // tm,),
        in_specs=[
            pl.BlockSpec((tm, D_MODEL), tok),
            pl.BlockSpec((tm, D_SSD), tok),
            pl.BlockSpec((tm, D_RET), tok),
            pl.BlockSpec((tm, D_MLA), tok),
            _mod_spec(layer, None if shared_cond else seq_len // tm),
            pl.BlockSpec((None, D_SSD, D_MODEL), lambda i: (layer, 0, 0), pipeline_mode=once),
            pl.BlockSpec((None, D_RET, D_MODEL), lambda i: (layer, ret_blk, 0), pipeline_mode=once),
            pl.BlockSpec((None, D_MLA, D_MODEL), lambda i: (layer, ret_blk + 1, 0), pipeline_mode=once),
            _layer_spec(layer, 1, D_MODEL),
            pl.BlockSpec((None, D_MODEL, 2 * D_FF), lambda i: (layer, 0, 0), pipeline_mode=once),
            pl.BlockSpec((None, D_FF, D_MODEL), lambda i: (layer, 0, 0), pipeline_mode=once),
            pl.BlockSpec((1, D_MODEL), lambda i: (0, 0)),
        ],
        out_specs=pl.BlockSpec((tm, D_MODEL), tok),
        out_shape=jax.ShapeDtypeStruct((t, D_MODEL), F32),
        compiler_params=pltpu.CompilerParams(
            dimension_semantics=("arbitrary",), vmem_limit_bytes=VMEM_LIMIT),
        name="out_ffn",
    )(x, mssd, mret, mmla, mod, wo, wo, wo, n2w, w1, w2, fnw)


def _pad_lanes(a, width):
    return jnp.pad(a, [(0, 0)] * (a.ndim - 1) + [(0, width - a.shape[-1])])


def _rope_tables(n_tokens):
    n_rows = n_tokens // GRID_W
    row, col = jnp.meshgrid(jnp.arange(n_rows), jnp.arange(GRID_W), indexing='ij')
    row = row.reshape(-1).astype(F32)
    col = col.reshape(-1).astype(F32)
    half = ROPE_DIM // 2
    inv = ROPE_BASE ** (-jnp.arange(0, half, 2, dtype=F32) / half)
    ang_r = row[:, None] * inv
    ang_c = col[:, None] * inv
    ang = jnp.concatenate([ang_r, ang_r, ang_c, ang_c], axis=-1)
    cos = jnp.concatenate([jnp.cos(ang), jnp.ones((n_tokens, LANES - ROPE_DIM), F32)], axis=-1)
    sin = _pad_lanes(jnp.sin(ang), LANES)
    return cos, sin


def kernel(x_prompt, x_sample, c, state_ssd, state_ret, cache_mla_ckv, cache_mla_krope, c_ctx,
           w_ada, b_ada, norm1_w, w_in, ssd_conv_w, ssd_conv_b, ssd_dt_bias, ssd_A_log, ssd_D, ssd_norm_w,
           ret_decay_logit, ret_gn_w, mla_q_norm_w, mla_w_uq, mla_kv_norm_w, mla_w_ukv, w_out, norm2_w,
           ffn_w1, ffn_w2, final_norm_w):
    batch, seq, _ = x_prompt.shape
    dec_batch, dec_seq, _ = x_sample.shape

    ret0 = SSD_PROJ
    mla0 = SSD_PROJ + RET_PROJ
    w_ssd = _pad_lanes(w_in[..., 0:SSD_PROJ], U_SSD).astype(BF16)
    w_ret = w_in[..., ret0:mla0].astype(BF16)
    w_mla = _pad_lanes(w_in[..., mla0:], U_MLA).astype(BF16)
    uq = mla_w_uq.reshape(DEPTH, Q_RANK, H_MLA, NOPE_DIM + ROPE_DIM).transpose(0, 2, 1, 3)
    def heads_side_by_side(w):
        return w.transpose(0, 2, 1, 3).reshape(DEPTH, w.shape[2], w.shape[1] * LANES).astype(BF16)

    wq = heads_side_by_side(_pad_lanes(jnp.concatenate([uq[..., NOPE_DIM:], uq[..., :NOPE_DIM]], axis=-1), LANES))
    qr = uq[..., NOPE_DIM:].reshape(DEPTH, H_MLA, Q_RANK, 2, 2, ROPE_DIM // 4)
    wq_rot = jnp.stack([-qr[..., 1, :], qr[..., 0, :]], axis=-2).reshape(DEPTH, H_MLA, Q_RANK, ROPE_DIM)
    wq_rot = heads_side_by_side(_pad_lanes(wq_rot, LANES))
    ukv = mla_w_ukv.reshape(DEPTH, KV_RANK, H_MLA, NOPE_DIM + V_DIM).transpose(0, 2, 1, 3)
    wk = heads_side_by_side(
        jnp.pad(ukv[..., :NOPE_DIM], [(0, 0)] * 3 + [(ROPE_DIM, LANES - ROPE_DIM - NOPE_DIM)]))
    wv = ukv[..., NOPE_DIM:].transpose(0, 2, 1, 3).reshape(DEPTH, KV_RANK, D_MLA).astype(BF16)
    wo = w_out.astype(BF16)
    w1 = ffn_w1.astype(BF16)
    w2 = ffn_w2.astype(BF16)
    conv_w = jnp.pad(ssd_conv_w, [(0, 0), (0, 8 - CONV_W), (0, 0)])
    conv_b = ssd_conv_b.reshape(DEPTH, 1, CONV_DIM)
    dt_bias = _pad_lanes(ssd_dt_bias.reshape(DEPTH, 1, 2 * H_SSD), LANES)
    a_log = _pad_lanes(ssd_A_log.reshape(DEPTH, 1, 2 * H_SSD), LANES)
    d_skip = jnp.repeat(ssd_D, P_SSD, axis=-1).reshape(DEPTH, 1, D_SSD)
    ssd_nw = ssd_norm_w.reshape(DEPTH, 1, D_SSD)
    decay_logit = _pad_lanes(ret_decay_logit.reshape(DEPTH, 1, 2 * H_RET), LANES)
    gn_w = ret_gn_w.reshape(DEPTH, 1, D_RET)
    q_nw = mla_q_norm_w.reshape(DEPTH, 1, Q_RANK)
    kv_nw = mla_kv_norm_w.reshape(DEPTH, 1, KV_RANK)
    n1w = norm1_w.reshape(DEPTH, 1, D_MODEL)
    n2w = norm2_w.reshape(DEPTH, 1, D_MODEL)
    fnw = final_norm_w.reshape(1, D_MODEL)
    rope = _rope_tables(dec_seq)

    assert dec_batch < N_COND
    cond = jnp.concatenate([c_ctx[None, :], c, jnp.zeros((N_COND - 1 - dec_batch, D_MODEL), F32)], axis=0)
    mod = _modulation(cond, w_ada, b_ada).reshape(DEPTH * N_COND, 1, 6 * D_MODEL)

    def layer(i, x, seq_len, shared_cond, h0_ssd, h0_ret, cache, prev):
        final = i == DEPTH - 1
        y_ssd, s_ssd = _ssd(x, mod, seq_len, shared_cond, h0_ssd, i, prev[0], n1w, w_ssd,
                            conv_w, conv_b, dt_bias, a_log, d_skip, ssd_nw)
        y_ret, s_ret = _ret(x, mod, seq_len, shared_cond, h0_ret, i, prev[1], n1w, w_ret, decay_logit, gn_w)
        mla_out = _mla(x, mod, seq_len, shared_cond, cache, i, rope, prev[2], n1w, w_mla,
                       q_nw, kv_nw, wq, wq_rot, wk, wv)
        x = _out_ffn(x, y_ssd, y_ret, mla_out[0], mod, seq_len, shared_cond, i, wo, n2w, w1, w2, fnw, final)
        return x, (s_ssd, s_ret, mla_out[1:])

    xp = x_prompt.reshape(batch * seq, D_MODEL)
    stacked = (None, None, None)
    for i in range(DEPTH):
        xp, stacked = layer(i, xp, seq, True, None, None, None, stacked)

    xs = x_sample.reshape(dec_batch * dec_seq, D_MODEL)
    for i in range(DEPTH):
        xs, _ = layer(i, xs, dec_seq, False, state_ssd, state_ret, (cache_mla_ckv, cache_mla_krope),
                      (None, None, None))

    new_ssd, new_ret, (new_ckv, new_kr) = stacked
    return (xp.reshape(batch, seq, D_MODEL), xs.reshape(dec_batch, dec_seq, D_MODEL),
            new_ssd, new_ret, new_ckv, new_kr)
```

```python
import functools
import math

import jax
import jax.numpy as jnp
from jax import lax
from jax.experimental import pallas as pl
from jax.experimental.pallas import tpu as pltpu

F32 = jnp.float32
BF16 = jnp.bfloat16

D_MODEL = 1024
DEPTH = 4
GRID_W = 64
CHUNK = 128
LANES = 128
SUBLANES = 8
HALO = CHUNK // 2
EPS = 1e-6

H_SSD, P_SSD, N_SSD, G_SSD = 8, 64, 64, 2
D_SSD = H_SSD * P_SSD
CONV_W = 5
CONV_DIM = D_SSD + 2 * G_SSD * N_SSD
SSD_PROJ = D_SSD + CONV_DIM + 2 * H_SSD
U_SSD = D_SSD + CONV_DIM + LANES

H_RET, DK_RET, DV_RET = 4, 64, 64
D_RET = H_RET * DV_RET
RET_PROJ = 2 * H_RET * DK_RET + 2 * D_RET

H_MLA, Q_RANK, KV_RANK, NOPE_DIM, ROPE_DIM, V_DIM = 4, 256, 128, 64, 32, 64
D_MLA = H_MLA * V_DIM
U_MLA = Q_RANK + KV_RANK + LANES
ROPE_BASE = 10000.0

D_FF = 2816
FF_SPLITS = ((0, 1024), (1024, 2048), (2048, D_FF))
TOKEN_TILE = 512
PROJ_ROWS = 256
GROUP_TOKENS = 1024
N_COND = 16
VMEM_LIMIT = 56 * 1024 * 1024


def _dot(a, b):
    return jnp.dot(a, b, preferred_element_type=F32)


def _dot_nt(a, b):
    return lax.dot_general(a, b, (((1,), (1,)), ((), ())), preferred_element_type=F32)


def _split3(x):
    hi = x.astype(BF16)
    r1 = x - hi.astype(F32)
    mid = r1.astype(BF16)
    lo = (r1 - mid.astype(F32)).astype(BF16)
    return hi, mid, lo


def _dot_exact_rhs(ones_bf16, x):
    hi, mid, lo = _split3(x)
    return _dot(ones_bf16, hi) + _dot(ones_bf16, mid) + _dot(ones_bf16, lo)


def _dot_exact_lhs(x, ones_bf16):
    hi, mid, lo = _split3(x)
    return _dot(hi, ones_bf16) + _dot(mid, ones_bf16) + _dot(lo, ones_bf16)


def _rms(x, w):
    return x * lax.rsqrt(jnp.mean(x * x, axis=-1, keepdims=True) + EPS) * w


def _silu(x):
    h = 0.5 * x
    return h + h * jnp.tanh(h)


def _softplus(x):
    return jnp.maximum(x, 0.0) + jnp.log1p(jnp.exp(-jnp.abs(x)))


def _lane_half(shape):
    return lax.broadcasted_iota(jnp.int32, shape, len(shape) - 1) % LANES < (LANES // 2)


def _mod_kernel(c_ref, w_ref, b_ref, o_ref):
    s = _silu(c_ref[...]).astype(BF16)
    o_ref[0] = _dot(s, w_ref[0].astype(BF16)) + b_ref[0]


def _modulation(cond, w_ada, b_ada):
    rows = cond.shape[0]
    n = w_ada.shape[-1]
    tn = 1536
    return pl.pallas_call(
        _mod_kernel,
        grid=(DEPTH, n // tn),
        in_specs=[
            pl.BlockSpec((rows, D_MODEL), lambda l, j: (0, 0)),
            pl.BlockSpec((1, D_MODEL, tn), lambda l, j: (l, 0, j)),
            pl.BlockSpec((1, 1, tn), lambda l, j: (l, 0, j)),
        ],
        out_specs=pl.BlockSpec((1, rows, tn), lambda l, j: (l, 0, j)),
        out_shape=jax.ShapeDtypeStruct((DEPTH, rows, n), F32),
        compiler_params=pltpu.CompilerParams(
            dimension_semantics=("arbitrary", "arbitrary"), vmem_limit_bytes=VMEM_LIMIT),
        name="modulation",
    )(cond, w_ada, b_ada.reshape(DEPTH, 1, n))


def _mod_spec(layer, per_row_tiles):
    base = layer * N_COND
    if per_row_tiles is None:
        index = lambda i, *_: (base, 0, 0)
    else:
        index = lambda i, *_: (base + 1 + i // per_row_tiles, 0, 0)
    return pl.BlockSpec((1, 1, 6 * D_MODEL), index)


def _token_tile(seq_len, shared_cond):
    return TOKEN_TILE if shared_cond else min(TOKEN_TILE, seq_len)


def _project_in(x_ref, mod_ref, n1w_ref, w_ref, u_s):
    sh1 = mod_ref[0, :, 0:D_MODEL]
    sc1 = mod_ref[0, :, D_MODEL:2 * D_MODEL]
    for r in range(0, x_ref.shape[0], PROJ_ROWS):
        h = (_rms(x_ref[r:r + PROJ_ROWS, :], n1w_ref[...]) * (1.0 + sc1) + sh1).astype(BF16)
        u_s[r:r + PROJ_ROWS, :] = _dot(h, w_ref[...])
        yield


def _interleave(gens):
    gens = list(gens)
    while gens:
        for g in list(gens):
            try:
                next(g)
            except StopIteration:
                gens.remove(g)


def _ssd_kernel(*refs, seq_len, has_h0, has_prev, group):
    if has_h0:
        hout_ref = None
        (x_ref, mod_ref, n1w_ref, w_ref, h0_ref, cw_ref, cb_ref, dtb_ref, alog_ref, dsk_ref, nw_ref,
         y_ref, *scratch) = refs
    else:
        h0_ref = None
        refs = refs[:10] + refs[10 + has_prev:]
        (x_ref, mod_ref, n1w_ref, w_ref, cw_ref, cb_ref, dtb_ref, alog_ref, dsk_ref, nw_ref,
         y_ref, hout_ref, *scratch) = refs
    _interleave(
        _ssd_seq(x_ref.at[s], mod_ref, n1w_ref, w_ref, None if h0_ref is None else h0_ref.at[s, 0],
                 cw_ref, cb_ref, dtb_ref, alog_ref, dsk_ref, nw_ref, y_ref.at[s],
                 None if hout_ref is None else hout_ref.at[s], *[r.at[s] for r in scratch], seq_len=seq_len)
        for s in range(group))


def _ssd_seq(x_ref, mod_ref, n1w_ref, w_ref, h0_ref, cw_ref, cb_ref, dtb_ref, alog_ref, dsk_ref, nw_ref,
             y_ref, hout_ref, u_s, xpad_s, xbc_s, dt_s, la_s, y_s, st_s, *, seq_len):
    has_h0 = h0_ref is not None
    L = seq_len
    nc = L // CHUNK
    yield from _project_in(x_ref, mod_ref, n1w_ref, w_ref, u_s)

    xpad_s[0:HALO, :] = jnp.zeros((HALO, CONV_DIM), BF16)
    xpad_s[HALO + L:2 * HALO + L, :] = jnp.zeros((HALO, CONV_DIM), BF16)
    xpad_s[HALO:HALO + L, :] = u_s[:, D_SSD:D_SSD + CONV_DIM].astype(BF16)
    side_taps = [k for k in range(CONV_W) if k != CONV_W // 2]
    sr = lax.broadcasted_iota(jnp.int32, (len(side_taps) * CHUNK, 2 * CHUNK), 0)
    sc = lax.broadcasted_iota(jnp.int32, (len(side_taps) * CHUNK, 2 * CHUNK), 1)
    src_row = sr % CHUNK + HALO
    for i, k in enumerate(side_taps):
        src_row = src_row + jnp.where(sr // CHUNK == i, k - CONV_W // 2, 0)
    shift = jnp.where(sc == src_row, 1.0, 0.0).astype(BF16)

    for c in range(nc):
        base = c * CHUNK
        moved = _dot(shift, xpad_s[base:base + 2 * CHUNK, :])
        centre = u_s[base:base + CHUNK, D_SSD:D_SSD + CONV_DIM]
        acc = cb_ref[...] + centre * cw_ref[CONV_W // 2:CONV_W // 2 + 1, :]
        for i, k in enumerate(side_taps):
            acc = acc + moved[i * CHUNK:(i + 1) * CHUNK, :] * cw_ref[k:k + 1, :]
        xbc_s[base:base + CHUNK, :] = _silu(acc)
        yield

    dt = _softplus(u_s[:, D_SSD + CONV_DIM:U_SSD] + dtb_ref[...])
    dt_s[...] = dt
    la_s[...] = dt * (-jnp.exp(alog_ref[...]))
    y_s[...] = xbc_s[:, 0:D_SSD] * dsk_ref[...]

    if has_h0:
        for d in range(2):
            for h in range(H_SSD):
                g, hh = divmod(h, H_SSD // G_SSD)
                st_s[d, g * N_SSD:(g + 1) * N_SSD, hh * P_SSD:(hh + 1) * P_SSD] = h0_ref[d, h]
    else:
        st_s[...] = jnp.zeros(st_s.shape, F32)

    ri = lax.broadcasted_iota(jnp.int32, (CHUNK, CHUNK), 0)
    ci = lax.broadcasted_iota(jnp.int32, (CHUNK, CHUNK), 1)
    first_group_lanes = _lane_half((CHUNK, LANES))
    first_group_rows = lax.broadcasted_iota(jnp.int32, (2 * N_SSD, 4 * P_SSD), 0) < N_SSD
    half_row = _lane_half((1, LANES))
    er = lax.broadcasted_iota(jnp.int32, (LANES, D_SSD), 0)
    el = lax.broadcasted_iota(jnp.int32, (LANES, D_SSD), 1) // P_SSD
    expand = [jnp.where(er == d * H_SSD + el, 1.0, 0.0).astype(BF16) for d in range(2)]

    keeps = [ri >= ci, ri <= ci]
    tris = [jnp.where(kp, 1.0, 0.0).astype(BF16) for kp in keeps]

    def stage_a(start, d):
        rows = pl.ds(start, CHUNK)
        cs = _dot_exact_rhs(tris[d], la_s[rows, :])
        tot = cs[CHUNK - 1:CHUNK, :] if d == 0 else cs[0:1, :]
        dt_wide = _dot_exact_lhs(dt_s[rows, :], expand[d])
        tot_wide = _dot_exact_lhs(jnp.broadcast_to(tot, (SUBLANES, LANES)), expand[d])[0:1, :]
        bmat = xbc_s[rows, D_SSD:D_SSD + LANES]
        cmat = xbc_s[rows, D_SSD + LANES:D_SSD + 2 * LANES]
        b_bf = bmat.astype(BF16)
        c_grp = [jnp.where(first_group_lanes, cmat, 0.0).astype(BF16),
                 jnp.where(first_group_lanes, 0.0, cmat).astype(BF16)]
        gmat = [_dot_nt(c_grp[g], b_bf) for g in range(G_SSD)]
        st_bf = st_s[d].astype(BF16)
        y_inter = [_dot(c_grp[g], st_bf) for g in range(G_SSD)]
        return dict(rows=rows, d=d, cs=cs, cst=cs.T, tot=tot, dt_wide=dt_wide, tot_wide=tot_wide,
                    gmat=gmat, y_inter=y_inter, bt=bmat.T.astype(BF16))

    def stage_b(t):
        rows, d, cs, cst, tot, dt_wide = t["rows"], t["d"], t["cs"], t["cst"], t["tot"], t["dt_wide"]
        gmat, y_inter, keep = t["gmat"], t["y_inter"], keeps[t["d"]]
        vte = []
        for pr in range(H_SSD // 2):
            g = pr // 2
            ia = d * H_SSD + 2 * pr
            ib = ia + 1
            pair = slice(pr * LANES, (pr + 1) * LANES)
            v = xbc_s[rows, pair] * dt_wide[:, pair]
            va = jnp.where(first_group_lanes, v, 0.0).astype(BF16)
            vb = jnp.where(first_group_lanes, 0.0, v).astype(BF16)
            col_a = jnp.broadcast_to(cs[:, ia:ia + 1], (CHUNK, LANES))
            col_b = jnp.broadcast_to(cs[:, ib:ib + 1], (CHUNK, LANES))
            seg_a = col_a - cst[ia:ia + 1, :]
            seg_b = col_b - cst[ib:ib + 1, :]
            s_a = (gmat[g] * jnp.exp(jnp.where(keep, seg_a, -jnp.inf))).astype(BF16)
            s_b = (gmat[g] * jnp.exp(jnp.where(keep, seg_b, -jnp.inf))).astype(BF16)
            y_intra = _dot(s_a, va) + _dot(s_b, vb)
            col_pair = jnp.where(first_group_lanes, col_a, col_b)
            tot_pair = jnp.where(half_row, tot[:, ia:ia + 1], tot[:, ib:ib + 1])
            lo = (pr % 2) * LANES
            y_pair = y_intra + y_inter[g][:, lo:lo + LANES] * jnp.exp(col_pair)
            y_s[rows, pair] += y_pair
            vte.append((v * jnp.exp(tot_pair - col_pair)).astype(BF16))

        bt = t["bt"]
        new0 = _dot(bt, jnp.concatenate([vte[0], vte[1]], axis=1))
        new1 = _dot(bt, jnp.concatenate([vte[2], vte[3]], axis=1))
        new = jnp.where(first_group_rows, new0, new1)
        cd = jnp.exp(t["tot_wide"])
        half_w = 4 * P_SSD
        decay = jnp.where(first_group_rows, cd[:, 0:half_w], cd[:, half_w:2 * half_w])
        st_s[d] = st_s[d] * decay + new

    tasks = [((c if d == 0 else nc - 1 - c) * CHUNK, d) for c in range(nc) for d in range(2)]
    yield
    pending = stage_a(*tasks[0])
    for nxt in tasks[1:]:
        yield
        upcoming = stage_a(*nxt)
        yield
        stage_b(pending)
        pending = upcoming
    yield
    stage_b(pending)
    yield

    if hout_ref is not None:
        for d in range(2):
            for h in range(H_SSD):
                g, hh = divmod(h, H_SSD // G_SSD)
                hout_ref[d, h] = st_s[d, g * N_SSD:(g + 1) * N_SSD, hh * P_SSD:(hh + 1) * P_SSD]

    y = y_s[...] * _silu(u_s[:, 0:D_SSD])
    y_ref[...] = _rms(y, nw_ref[...])


def _layer_spec(layer, *tail):
    zeros = (0,) * len(tail)
    return pl.BlockSpec((None,) + tail, lambda *_: (layer,) + zeros)


def _state_out(nb, layer, prev, *tail, group=1):
    zeros = (0,) * len(tail)
    spec = pl.BlockSpec((group, None) + tail, lambda b: (b, layer) + zeros)
    shape = jax.ShapeDtypeStruct((nb, DEPTH) + tail, F32)
    return spec, shape, ([] if prev is None else [prev])


def _seq_group(seq_len, shared_cond):
    return max(1, GROUP_TOKENS // seq_len) if shared_cond else 1


def _seq_inputs(x, mod, seq_len, shared_cond, layer, n1w, w, width, group=1):
    nb = x.shape[0] // seq_len
    assert nb % group == 0 and (shared_cond or group == 1)
    specs = [pl.BlockSpec((group, seq_len, D_MODEL), lambda b: (b, 0, 0)),
             _mod_spec(layer, None if shared_cond else 1),
             _layer_spec(layer, 1, D_MODEL),
             _layer_spec(layer, D_MODEL, width)]
    return nb, specs, [x.reshape(nb, seq_len, D_MODEL), mod, n1w, w]


def _ssd(x, mod, seq_len, shared_cond, h0, layer, prev_states, n1w, w, cw, cb, dtb, alog, dsk, nw):
    grp = _seq_group(seq_len, shared_cond)
    nb, in_specs, args = _seq_inputs(x, mod, seq_len, shared_cond, layer, n1w, w, U_SSD, grp)
    has_h0 = h0 is not None
    if has_h0:
        in_specs.append(pl.BlockSpec((grp, 1, 2, H_SSD, N_SSD, P_SSD), lambda b: (b, layer, 0, 0, 0, 0)))
        args.append(h0)
    in_specs += [
        _layer_spec(layer, 8, CONV_DIM),
        _layer_spec(layer, 1, CONV_DIM),
        _layer_spec(layer, 1, LANES),
        _layer_spec(layer, 1, LANES),
        _layer_spec(layer, 1, D_SSD),
        _layer_spec(layer, 1, D_SSD),
    ]
    args += [cw, cb, dtb, alog, dsk, nw]
    out_specs = [pl.BlockSpec((grp, seq_len, D_SSD), lambda b: (b, 0, 0))]
    out_shape = [jax.ShapeDtypeStruct((nb, seq_len, D_SSD), F32)]
    aliases = {}
    if not has_h0:
        spec, shape, prev = _state_out(nb, layer, prev_states, 2, H_SSD, N_SSD, P_SSD, group=grp)
        out_specs.append(spec)
        out_shape.append(shape)
        if prev:
            aliases = {len(args): 1}
            in_specs.append(pl.BlockSpec(memory_space=pl.ANY))
            args += prev
    outs = pl.pallas_call(
        functools.partial(_ssd_kernel, seq_len=seq_len, has_h0=has_h0, has_prev=bool(aliases), group=grp),
        grid=(nb // grp,),
        in_specs=in_specs,
        out_specs=out_specs,
        out_shape=out_shape,
        input_output_aliases=aliases,
        scratch_shapes=[
            pltpu.VMEM((grp, seq_len, U_SSD), F32),
            pltpu.VMEM((grp, seq_len + 2 * HALO, CONV_DIM), BF16),
            pltpu.VMEM((grp, seq_len, CONV_DIM), F32),
            pltpu.VMEM((grp, seq_len, LANES), F32),
            pltpu.VMEM((grp, seq_len, LANES), F32),
            pltpu.VMEM((grp, seq_len, D_SSD), F32),
            pltpu.VMEM((grp, 2, 2 * N_SSD, 4 * P_SSD), F32),
        ],
        compiler_params=pltpu.CompilerParams(
            dimension_semantics=("arbitrary",), vmem_limit_bytes=VMEM_LIMIT),
        name="ssd",
    )(*args)
    return outs[0].reshape(nb * seq_len, D_SSD), (None if has_h0 else outs[1])


def _ret_kernel(*refs, seq_len, has_h0, has_prev, group):
    if has_h0:
        hout_ref = None
        x_ref, mod_ref, n1w_ref, w_ref, h0_ref, dl_ref, gnw_ref, y_ref, *scratch = refs
    else:
        h0_ref = None
        refs = refs[:6] + refs[6 + has_prev:]
        x_ref, mod_ref, n1w_ref, w_ref, dl_ref, gnw_ref, y_ref, hout_ref, *scratch = refs
    _interleave(
        _ret_seq(x_ref.at[s], mod_ref, n1w_ref, w_ref, None if h0_ref is None else h0_ref.at[s, 0],
                 dl_ref, gnw_ref, y_ref.at[s], None if hout_ref is None else hout_ref.at[s],
                 *[r.at[s] for r in scratch], seq_len=seq_len)
        for s in range(group))


def _ret_seq(x_ref, mod_ref, n1w_ref, w_ref, h0_ref, dl_ref, gnw_ref, y_ref, hout_ref, u_s, y_s, st_s,
             *, seq_len):
    has_h0 = h0_ref is not None
    yield from _project_in(x_ref, mod_ref, n1w_ref, w_ref, u_s)
    L = seq_len
    nc = L // CHUNK
    npair = H_RET // 2
    nq = H_RET * DK_RET

    lg_row = -_softplus(-dl_ref[...])
    ri = lax.broadcasted_iota(jnp.int32, (CHUNK, CHUNK), 0)
    ci = lax.broadcasted_iota(jnp.int32, (CHUNK, CHUNK), 1)
    rif = ri.astype(F32)
    half = _lane_half((CHUNK, LANES))
    block_diag = (ri < DK_RET) == (ci < DV_RET)

    if has_h0:
        st_s[...] = jnp.zeros(st_s.shape, F32)
        for d in range(2):
            for h in range(H_RET):
                pr, hh = divmod(h, 2)
                st_s[d, pr, hh * DK_RET:(hh + 1) * DK_RET, hh * DV_RET:(hh + 1) * DV_RET] = h0_ref[d, h]
    else:
        st_s[...] = jnp.zeros(st_s.shape, F32)
    y_s[...] = jnp.zeros(y_s.shape, F32)

    consts = {}
    for d in range(2):
        keep = (ri >= ci) if d == 0 else (ri <= ci)
        dist = (ri - ci).astype(F32) if d == 0 else (ci - ri).astype(F32)
        for pr in range(npair):
            ia = d * H_RET + 2 * pr
            lg_a = lg_row[:, ia:ia + 1]
            lg_b = lg_row[:, ia + 1:ia + 2]
            lg_pair = jnp.where(half, lg_a, lg_b)
            if d == 0:
                in_scale = jnp.exp((rif + 1.0) * lg_pair)
                to_end = jnp.exp((CHUNK - 1.0 - rif) * lg_pair)
            else:
                in_scale = jnp.exp((CHUNK - rif) * lg_pair)
                to_end = jnp.exp(rif * lg_pair)
            consts[d, pr] = dict(
                dec_a=jnp.exp(jnp.where(keep, dist * lg_a, -jnp.inf)),
                dec_b=jnp.exp(jnp.where(keep, dist * lg_b, -jnp.inf)),
                in_scale=in_scale, to_end=to_end, cd=jnp.exp(float(CHUNK) * lg_pair))

    def stage_a(start, d, pr):
        cst = consts[d, pr]
        rows = pl.ds(start, CHUNK)
        lanes = slice(pr * LANES, (pr + 1) * LANES)
        q = u_s[rows, lanes]
        k = u_s[rows, nq + pr * LANES:nq + (pr + 1) * LANES] * (DK_RET ** -0.5)
        v = u_s[rows, 2 * nq + pr * LANES:2 * nq + (pr + 1) * LANES]
        k_bf = k.astype(BF16)
        qk_a = _dot_nt(jnp.where(half, q, 0.0).astype(BF16), k_bf)
        qk_b = _dot_nt(jnp.where(half, 0.0, q).astype(BF16), k_bf)
        st = st_s[d, pr]
        y_inter = _dot(q.astype(BF16), st.astype(BF16)) * cst["in_scale"]
        new = _dot(k.T.astype(BF16), (v * cst["to_end"]).astype(BF16))
        st_s[d, pr] = st * cst["cd"] + jnp.where(block_diag, new, 0.0)
        return dict(rows=rows, lanes=lanes, cst=cst, qk_a=qk_a, qk_b=qk_b, y_inter=y_inter,
                    va=jnp.where(half, v, 0.0).astype(BF16), vb=jnp.where(half, 0.0, v).astype(BF16))

    def stage_b(t):
        s_a = (t["qk_a"] * t["cst"]["dec_a"]).astype(BF16)
        s_b = (t["qk_b"] * t["cst"]["dec_b"]).astype(BF16)
        y_s[t["rows"], t["lanes"]] += _dot(s_a, t["va"]) + _dot(s_b, t["vb"]) + t["y_inter"]

    tasks = [((c if d == 0 else nc - 1 - c) * CHUNK, d, pr)
             for c in range(nc) for d in range(2) for pr in range(npair)]
    yield
    pending = stage_a(*tasks[0])
    for nxt in tasks[1:]:
        yield
        upcoming = stage_a(*nxt)
        yield
        stage_b(pending)
        pending = upcoming
    yield
    stage_b(pending)
    yield

    if hout_ref is not None:
        for d in range(2):
            for h in range(H_RET):
                pr, hh = divmod(h, 2)
                hout_ref[d, h] = st_s[d, pr, hh * DK_RET:(hh + 1) * DK_RET, hh * DV_RET:(hh + 1) * DV_RET]

    half_l = _lane_half((L, LANES))
    for pr in range(npair):
        lanes = slice(pr * LANES, (pr + 1) * LANES)
        o = y_s[:, lanes]
        inv = 1.0 / DV_RET
        sum_a = jnp.sum(jnp.where(half_l, o, 0.0), axis=-1, keepdims=True)
        sum_all = jnp.sum(o, axis=-1, keepdims=True)
        mu = jnp.where(half_l, sum_a, sum_all - sum_a) * inv
        c = o - mu
        c2 = c * c
        sq_a = jnp.sum(jnp.where(half_l, c2, 0.0), axis=-1, keepdims=True)
        sq_all = jnp.sum(c2, axis=-1, keepdims=True)
        var = jnp.where(half_l, sq_a, sq_all - sq_a) * inv
        on = c * lax.rsqrt(var + EPS) * gnw_ref[:, lanes]
        gate = u_s[:, 2 * nq + D_RET + pr * LANES:2 * nq + D_RET + (pr + 1) * LANES]
        y_ref[:, lanes] = _silu(gate) * on
        yield


def _ret(x, mod, seq_len, shared_cond, h0, layer, prev_states, n1w, w, dl, gnw):
    grp = _seq_group(seq_len, shared_cond)
    nb, in_specs, args = _seq_inputs(x, mod, seq_len, shared_cond, layer, n1w, w, RET_PROJ, grp)
    has_h0 = h0 is not None
    if has_h0:
        in_specs.append(pl.BlockSpec((grp, 1, 2, H_RET, DK_RET, DV_RET), lambda b: (b, layer, 0, 0, 0, 0)))
        args.append(h0)
    in_specs += [_layer_spec(layer, 1, LANES), _layer_spec(layer, 1, D_RET)]
    args += [dl, gnw]
    out_specs = [pl.BlockSpec((grp, seq_len, D_RET), lambda b: (b, 0, 0))]
    out_shape = [jax.ShapeDtypeStruct((nb, seq_len, D_RET), F32)]
    aliases = {}
    if not has_h0:
        spec, shape, prev = _state_out(nb, layer, prev_states, 2, H_RET, DK_RET, DV_RET, group=grp)
        out_specs.append(spec)
        out_shape.append(shape)
        if prev:
            aliases = {len(args): 1}
            in_specs.append(pl.BlockSpec(memory_space=pl.ANY))
            args += prev
    outs = pl.pallas_call(
        functools.partial(_ret_kernel, seq_len=seq_len, has_h0=has_h0, has_prev=bool(aliases), group=grp),
        grid=(nb // grp,),
        in_specs=in_specs,
        out_specs=out_specs,
        out_shape=out_shape,
        input_output_aliases=aliases,
        scratch_shapes=[
            pltpu.VMEM((grp, seq_len, RET_PROJ), F32),
            pltpu.VMEM((grp, seq_len, D_RET), F32),
            pltpu.VMEM((grp, 2, H_RET // 2, 2 * DK_RET, 2 * DV_RET), F32),
        ],
        compiler_params=pltpu.CompilerParams(
            dimension_semantics=("arbitrary",), vmem_limit_bytes=VMEM_LIMIT),
        name="retention",
    )(*args)
    return outs[0].reshape(nb * seq_len, D_RET), (None if has_h0 else outs[1])


def _rot_matrix():
    r = lax.broadcasted_iota(jnp.int32, (LANES, LANES), 0)
    l = lax.broadcasted_iota(jnp.int32, (LANES, LANES), 1)
    even = (l // 8) % 2 == 0
    rope = l < ROPE_DIM
    p = jnp.where(rope & even & (r == l + 8), -1.0, 0.0) + jnp.where(rope & ~even & (r == l - 8), 1.0, 0.0)
    return p.astype(BF16)


def _mla_kernel(*refs, seq_len, past_len, n_prev, group):
    if past_len:
        (x_ref, mod_ref, n1w_ref, w_ref, pckv_ref, pkr_ref, cos_ref, sin_ref, qnw_ref, kvnw_ref,
         wq_ref, wqr_ref, wk_ref, wv_ref, o_ref, *scratch) = refs
        ckv_ref = kr_ref = None
    else:
        refs = refs[:9] + refs[9 + n_prev:]
        (x_ref, mod_ref, n1w_ref, w_ref, qnw_ref, kvnw_ref, wq_ref, wk_ref, wv_ref,
         o_ref, ckv_ref, kr_ref, *scratch) = refs
        pckv_ref = pkr_ref = cos_ref = sin_ref = wqr_ref = None
    at = lambda ref, *idx: None if ref is None else ref.at[idx]
    _interleave(
        _mla_seq(x_ref.at[s], mod_ref, n1w_ref, w_ref, at(pckv_ref, s, 0), at(pkr_ref, s, 0), cos_ref, sin_ref,
                 qnw_ref, kvnw_ref, wq_ref, wqr_ref, wk_ref, wv_ref, o_ref.at[s], at(ckv_ref, s), at(kr_ref, s),
                 *[r.at[s] for r in scratch], seq_len=seq_len, past_len=past_len)
        for s in range(group))


def _mla_seq(x_ref, mod_ref, n1w_ref, w_ref, pckv_ref, pkr_ref, cos_ref, sin_ref, qnw_ref, kvnw_ref,
             wq_ref, wqr_ref, wk_ref, wv_ref, o_ref, ckv_ref, kr_ref, u_s, q_s, k_s, v_s, ckv_s, kr_s,
             *, seq_len, past_len):
    yield from _project_in(x_ref, mod_ref, n1w_ref, w_ref, u_s)
    L = seq_len
    S = past_len + L
    nqb = L // CHUNK
    qscale = (NOPE_DIM + ROPE_DIM) ** -0.5 * math.log2(math.e)

    q_c = _rms(u_s[:, 0:Q_RANK], qnw_ref[...]).astype(BF16)
    ckv = _rms(u_s[:, Q_RANK:Q_RANK + KV_RANK], kvnw_ref[...])
    kr = u_s[:, Q_RANK + KV_RANK:U_MLA]
    if past_len:
        cos = cos_ref[...]
        sin = sin_ref[...]
        ckv_s[0:past_len, :] = pckv_ref[...]
        kr_s[0:past_len, :] = jnp.zeros((past_len, LANES), F32)
        kr_s[0:past_len, 0:ROPE_DIM] = pkr_ref[...]
        kr_s[past_len:S, :] = kr * cos + _dot_exact_lhs(kr, _rot_matrix()) * sin
    else:
        ckv_ref[...] = ckv
        kr_ref[...] = kr[:, 0:ROPE_DIM]
        kr_s[...] = kr
    ckv_s[past_len:S, :] = ckv

    ckv_all = ckv_s[...].astype(BF16)
    kr_all = kr_s[...]
    half_s = _lane_half((S, LANES))
    q_all = _dot(q_c, wq_ref[...])
    k_all = _dot(ckv_all, wk_ref[...])
    v_all = _dot(ckv_all, wv_ref[...])
    if past_len:
        qr_all = _dot(q_c, wqr_ref[...])
    for h in range(H_MLA):
        slab = slice(h * LANES, (h + 1) * LANES)
        qh = q_all[:, slab]
        if past_len:
            qh = qh * cos + qr_all[:, slab] * sin
        q_s[h] = (qh * qscale).astype(BF16)
        k_s[h] = (k_all[:, slab] + kr_all).astype(BF16)
    for pr in range(H_MLA // 2):
        vp = v_all[:, pr * LANES:(pr + 1) * LANES]
        v_s[2 * pr] = jnp.where(half_s, vp, 0.0).astype(BF16)
        v_s[2 * pr + 1] = jnp.where(half_s, 0.0, vp).astype(BF16)

    def scores(t):
        qb, h = divmod(t, H_MLA)
        return _dot_nt(q_s[h, qb * CHUNK:(qb + 1) * CHUNK, :], k_s[h])

    ntask = nqb * H_MLA
    yield
    s_next = scores(0)
    first = None
    for t in range(ntask):
        qb, h = divmod(t, H_MLA)
        s = s_next
        if t + 1 < ntask:
            s_next = scores(t + 1)
        e = jnp.exp2(s - jnp.max(s, axis=-1, keepdims=True))
        inv = 1.0 / jnp.sum(e, axis=-1, keepdims=True)
        o = _dot(e.astype(BF16), v_s[h]) * inv
        if h % 2 == 0:
            first = o
        else:
            pr = h // 2
            o_ref[qb * CHUNK:(qb + 1) * CHUNK, pr * LANES:(pr + 1) * LANES] = first + o
        yield


def _mla(x, mod, seq_len, shared_cond, cache, layer, rope, prev_cache, n1w, w, qnw, kvnw, wq, wqr, wk, wv):
    grp = _seq_group(seq_len, shared_cond)
    nb, in_specs, args = _seq_inputs(x, mod, seq_len, shared_cond, layer, n1w, w, U_MLA, grp)
    past_len = cache[0].shape[2] if cache is not None else 0
    s_len = past_len + seq_len
    full2 = lambda b: (0, 0)
    if past_len:
        in_specs += [
            pl.BlockSpec((grp, 1, past_len, KV_RANK), lambda b: (b, layer, 0, 0)),
            pl.BlockSpec((grp, 1, past_len, ROPE_DIM), lambda b: (b, layer, 0, 0)),
            pl.BlockSpec((seq_len, LANES), full2),
            pl.BlockSpec((seq_len, LANES), full2),
        ]
        args += [cache[0], cache[1], rope[0], rope[1]]
    in_specs += [_layer_spec(layer, 1, Q_RANK), _layer_spec(layer, 1, KV_RANK),
                 _layer_spec(layer, Q_RANK, H_MLA * LANES)]
    args += [qnw, kvnw, wq]
    if past_len:
        in_specs.append(_layer_spec(layer, Q_RANK, H_MLA * LANES))
        args.append(wqr)
    in_specs += [_layer_spec(layer, KV_RANK, H_MLA * LANES), _layer_spec(layer, KV_RANK, D_MLA)]
    args += [wk, wv]
    out_specs = [pl.BlockSpec((grp, seq_len, D_MLA), lambda b: (b, 0, 0))]
    out_shape = [jax.ShapeDtypeStruct((nb, seq_len, D_MLA), F32)]
    aliases = {}
    if not past_len:
        for i, width in enumerate((KV_RANK, ROPE_DIM)):
            spec, shape, prev = _state_out(nb, layer, None if prev_cache is None else prev_cache[i],
                                           seq_len, width, group=grp)
            out_specs.append(spec)
            out_shape.append(shape)
            if prev:
                aliases[len(args)] = 1 + i
                in_specs.append(pl.BlockSpec(memory_space=pl.ANY))
                args += prev
    outs = pl.pallas_call(
        functools.partial(_mla_kernel, seq_len=seq_len, past_len=past_len, n_prev=len(aliases), group=grp),
        grid=(nb // grp,),
        in_specs=in_specs,
        out_specs=out_specs,
        out_shape=out_shape,
        input_output_aliases=aliases,
        scratch_shapes=[
            pltpu.VMEM((grp, seq_len, U_MLA), F32),
            pltpu.VMEM((grp, H_MLA, seq_len, LANES), BF16),
            pltpu.VMEM((grp, H_MLA, s_len, LANES), BF16),
            pltpu.VMEM((grp, H_MLA, s_len, LANES), BF16),
            pltpu.VMEM((grp, s_len, KV_RANK), F32),
            pltpu.VMEM((grp, s_len, LANES), F32),
        ],
        compiler_params=pltpu.CompilerParams(
            dimension_semantics=("arbitrary",), vmem_limit_bytes=VMEM_LIMIT),
        name="mla",
    )(*args)
    o = outs[0].reshape(nb * seq_len, D_MLA)
    return (o,) + tuple(outs[1:])


def _out_ffn_kernel(x_ref, mssd_ref, mret_ref, mmla_ref, mod_ref, wos_ref, wor_ref, wom_ref, n2w_ref,
                    w1_ref, w2_ref, fnw_ref, o_ref, *, final):
    mix = (_dot(mssd_ref[...].astype(BF16), wos_ref[...])
           + _dot(mret_ref[...].astype(BF16), wor_ref[...])
           + _dot(mmla_ref[...].astype(BF16), wom_ref[...]))
    g1 = mod_ref[0, :, 2 * D_MODEL:3 * D_MODEL]
    sh2 = mod_ref[0, :, 3 * D_MODEL:4 * D_MODEL]
    sc2 = mod_ref[0, :, 4 * D_MODEL:5 * D_MODEL]
    g2 = mod_ref[0, :, 5 * D_MODEL:6 * D_MODEL]
    x1 = x_ref[...] + g1 * mix
    h2 = (_rms(x1, n2w_ref[...]) * (1.0 + sc2) + sh2).astype(BF16)

    def up(c):
        lo, hi = FF_SPLITS[c]
        return _dot(h2, w1_ref[:, lo:hi]), _dot(h2, w1_ref[:, D_FF + lo:D_FF + hi])

    acc = None
    nxt = up(0)
    for c, (lo, hi) in enumerate(FF_SPLITS):
        a, gt = nxt
        if c + 1 < len(FF_SPLITS):
            nxt = up(c + 1)
        down = _dot((_silu(a) * gt).astype(BF16), w2_ref[lo:hi, :])
        acc = down if acc is None else acc + down
    out = x1 + g2 * acc
    if final:
        out = _rms(out, fnw_ref[...])
    o_ref[...] = out


def _out_ffn(x, mssd, mret, mmla, mod, seq_len, shared_cond, layer, wo, n2w, w1, w2, fnw, final):
    t = x.shape[0]
    tm = _token_tile(seq_len, shared_cond)
    tok = lambda i: (i, 0)
    ret_blk = D_SSD // D_RET
    once = pl.Buffered(1)
    return pl.pallas_call(
        functools.partial(_out_ffn_kernel, final=final),
        grid=(t // tm,),
        in_specs=[
            pl.BlockSpec((tm, D_MODEL), tok),
            pl.BlockSpec((tm, D_SSD), tok),
            pl.BlockSpec((tm, D_RET), tok),
            pl.BlockSpec((tm, D_MLA), tok),
            _mod_spec(layer, None if shared_cond else seq_len // tm),
            pl.BlockSpec((None, D_SSD, D_MODEL), lambda i: (layer, 0, 0), pipeline_mode=once),
            pl.BlockSpec((None, D_RET, D_MODEL), lambda i: (layer, ret_blk, 0), pipeline_mode=once),
            pl.BlockSpec((None, D_MLA, D_MODEL), lambda i: (layer, ret_blk + 1, 0), pipeline_mode=once),
            _layer_spec(layer, 1, D_MODEL),
            pl.BlockSpec((None, D_MODEL, 2 * D_FF), lambda i: (layer, 0, 0), pipeline_mode=once),
            pl.BlockSpec((None, D_FF, D_MODEL), lambda i: (layer, 0, 0), pipeline_mode=once),
            pl.BlockSpec((1, D_MODEL), lambda i: (0, 0)),
        ],
        out_specs=pl.BlockSpec((tm, D_MODEL), tok),
        out_shape=jax.ShapeDtypeStruct((t, D_MODEL), F32),
        compiler_params=pltpu.CompilerParams(
            dimension_semantics=("arbitrary",), vmem_limit_bytes=VMEM_LIMIT),
        name="out_ffn",
    )(x, mssd, mret, mmla, mod, wo, wo, wo, n2w, w1, w2, fnw)


def _pad_lanes(a, width):
    return jnp.pad(a, [(0, 0)] * (a.ndim - 1) + [(0, width - a.shape[-1])])


def _rope_tables(n_tokens):
    n_rows = n_tokens // GRID_W
    row, col = jnp.meshgrid(jnp.arange(n_rows), jnp.arange(GRID_W), indexing='ij')
    row = row.reshape(-1).astype(F32)
    col = col.reshape(-1).astype(F32)
    half = ROPE_DIM // 2
    inv = ROPE_BASE ** (-jnp.arange(0, half, 2, dtype=F32) / half)
    ang_r = row[:, None] * inv
    ang_c = col[:, None] * inv
    ang = jnp.concatenate([ang_r, ang_r, ang_c, ang_c], axis=-1)
    cos = jnp.concatenate([jnp.cos(ang), jnp.ones((n_tokens, LANES - ROPE_DIM), F32)], axis=-1)
    sin = _pad_lanes(jnp.sin(ang), LANES)
    return cos, sin


def kernel(x_prompt, x_sample, c, state_ssd, state_ret, cache_mla_ckv, cache_mla_krope, c_ctx,
           w_ada, b_ada, norm1_w, w_in, ssd_conv_w, ssd_conv_b, ssd_dt_bias, ssd_A_log, ssd_D, ssd_norm_w,
           ret_decay_logit, ret_gn_w, mla_q_norm_w, mla_w_uq, mla_kv_norm_w, mla_w_ukv, w_out, norm2_w,
           ffn_w1, ffn_w2, final_norm_w):
    batch, seq, _ = x_prompt.shape
    dec_batch, dec_seq, _ = x_sample.shape

    ret0 = SSD_PROJ
    mla0 = SSD_PROJ + RET_PROJ
    w_ssd = _pad_lanes(w_in[..., 0:SSD_PROJ], U_SSD).astype(BF16)
    w_ret = w_in[..., ret0:mla0].astype(BF16)
    w_mla = _pad_lanes(w_in[..., mla0:], U_MLA).astype(BF16)
    uq = mla_w_uq.reshape(DEPTH, Q_RANK, H_MLA, NOPE_DIM + ROPE_DIM).transpose(0, 2, 1, 3)
    def heads_side_by_side(w):
        return w.transpose(0, 2, 1, 3).reshape(DEPTH, w.shape[2], w.shape[1] * LANES).astype(BF16)

    wq = heads_side_by_side(_pad_lanes(jnp.concatenate([uq[..., NOPE_DIM:], uq[..., :NOPE_DIM]], axis=-1), LANES))
    qr = uq[..., NOPE_DIM:].reshape(DEPTH, H_MLA, Q_RANK, 2, 2, ROPE_DIM // 4)
    wq_rot = jnp.stack([-qr[..., 1, :], qr[..., 0, :]], axis=-2).reshape(DEPTH, H_MLA, Q_RANK, ROPE_DIM)
    wq_rot = heads_side_by_side(_pad_lanes(wq_rot, LANES))
    ukv = mla_w_ukv.reshape(DEPTH, KV_RANK, H_MLA, NOPE_DIM + V_DIM).transpose(0, 2, 1, 3)
    wk = heads_side_by_side(
        jnp.pad(ukv[..., :NOPE_DIM], [(0, 0)] * 3 + [(ROPE_DIM, LANES - ROPE_DIM - NOPE_DIM)]))
    wv = ukv[..., NOPE_DIM:].transpose(0, 2, 1, 3).reshape(DEPTH, KV_RANK, D_MLA).astype(BF16)
    wo = w_out.astype(BF16)
    w1 = ffn_w1.astype(BF16)
    w2 = ffn_w2.astype(BF16)
    conv_w = jnp.pad(ssd_conv_w, [(0, 0), (0, 8 - CONV_W), (0, 0)])
    conv_b = ssd_conv_b.reshape(DEPTH, 1, CONV_DIM)
    dt_bias = _pad_lanes(ssd_dt_bias.reshape(DEPTH, 1, 2 * H_SSD), LANES)
    a_log = _pad_lanes(ssd_A_log.reshape(DEPTH, 1, 2 * H_SSD), LANES)
    d_skip = jnp.repeat(ssd_D, P_SSD, axis=-1).reshape(DEPTH, 1, D_SSD)
    ssd_nw = ssd_norm_w.reshape(DEPTH, 1, D_SSD)
    decay_logit = _pad_lanes(ret_decay_logit.reshape(DEPTH, 1, 2 * H_RET), LANES)
    gn_w = ret_gn_w.reshape(DEPTH, 1, D_RET)
    q_nw = mla_q_norm_w.reshape(DEPTH, 1, Q_RANK)
    kv_nw = mla_kv_norm_w.reshape(DEPTH, 1, KV_RANK)
    n1w = norm1_w.reshape(DEPTH, 1, D_MODEL)
    n2w = norm2_w.reshape(DEPTH, 1, D_MODEL)
    fnw = final_norm_w.reshape(1, D_MODEL)
    rope = _rope_tables(dec_seq)

    assert dec_batch < N_COND
    cond = jnp.concatenate([c_ctx[None, :], c, jnp.zeros((N_COND - 1 - dec_batch, D_MODEL), F32)], axis=0)
    mod = _modulation(cond, w_ada, b_ada).reshape(DEPTH * N_COND, 1, 6 * D_MODEL)

    def layer(i, x, seq_len, shared_cond, h0_ssd, h0_ret, cache, prev):
        final = i == DEPTH - 1
        y_ssd, s_ssd = _ssd(x, mod, seq_len, shared_cond, h0_ssd, i, prev[0], n1w, w_ssd,
                            conv_w, conv_b, dt_bias, a_log, d_skip, ssd_nw)
        y_ret, s_ret = _ret(x, mod, seq_len, shared_cond, h0_ret, i, prev[1], n1w, w_ret, decay_logit, gn_w)
        mla_out = _mla(x, mod, seq_len, shared_cond, cache, i, rope, prev[2], n1w, w_mla,
                       q_nw, kv_nw, wq, wq_rot, wk, wv)
        x = _out_ffn(x, y_ssd, y_ret, mla_out[0], mod, seq_len, shared_cond, i, wo, n2w, w1, w2, fnw, final)
        return x, (s_ssd, s_ret, mla_out[1:])

    xp = x_prompt.reshape(batch * seq, D_MODEL)
    stacked = (None, None, None)
    for i in range(DEPTH):
        xp, stacked = layer(i, xp, seq, True, None, None, None, stacked)

    xs = x_sample.reshape(dec_batch * dec_seq, D_MODEL)
    for i in range(DEPTH):
        xs, _ = layer(i, xs, dec_seq, False, state_ssd, state_ret, (cache_mla_ckv, cache_mla_krope),
                      (None, None, None))

    new_ssd, new_ret, (new_ckv, new_kr) = stacked
    return (xp.reshape(batch, seq, D_MODEL), xs.reshape(dec_batch, dec_seq, D_MODEL),
            new_ssd, new_ret, new_ckv, new_kr)
```

```python
import functools
import math

import jax
import jax.numpy as jnp
from jax import lax
from jax.experimental import pallas as pl
from jax.experimental.pallas import tpu as pltpu

F32 = jnp.float32
BF16 = jnp.bfloat16

D_MODEL = 1024
DEPTH = 4
GRID_W = 64
CHUNK = 128
LANES = 128
SUBLANES = 8
HALO = CHUNK // 2
EPS = 1e-6

H_SSD, P_SSD, N_SSD, G_SSD = 8, 64, 64, 2
D_SSD = H_SSD * P_SSD
CONV_W = 5
CONV_DIM = D_SSD + 2 * G_SSD * N_SSD
SSD_PROJ = D_SSD + CONV_DIM + 2 * H_SSD
U_SSD = D_SSD + CONV_DIM + LANES

H_RET, DK_RET, DV_RET = 4, 64, 64
D_RET = H_RET * DV_RET
RET_PROJ = 2 * H_RET * DK_RET + 2 * D_RET

H_MLA, Q_RANK, KV_RANK, NOPE_DIM, ROPE_DIM, V_DIM = 4, 256, 128, 64, 32, 64
D_MLA = H_MLA * V_DIM
U_MLA = Q_RANK + KV_RANK + LANES
ROPE_BASE = 10000.0

D_FF = 2816
FF_SPLITS = ((0, 1024), (1024, 2048), (2048, D_FF))
TOKEN_TILE = 512
PROJ_ROWS = 256
GROUP_TOKENS = 1024
N_COND = 16
VMEM_LIMIT = 56 * 1024 * 1024


def _dot(a, b):
    return jnp.dot(a, b, preferred_element_type=F32)


def _dot_nt(a, b):
    return lax.dot_general(a, b, (((1,), (1,)), ((), ())), preferred_element_type=F32)


def _split3(x):
    hi = x.astype(BF16)
    r1 = x - hi.astype(F32)
    mid = r1.astype(BF16)
    lo = (r1 - mid.astype(F32)).astype(BF16)
    return hi, mid, lo


def _dot_exact_rhs(ones_bf16, x):
    hi, mid, lo = _split3(x)
    return _dot(ones_bf16, hi) + _dot(ones_bf16, mid) + _dot(ones_bf16, lo)


def _dot_exact_lhs(x, ones_bf16):
    hi, mid, lo = _split3(x)
    return _dot(hi, ones_bf16) + _dot(mid, ones_bf16) + _dot(lo, ones_bf16)


def _rms(x, w):
    return x * lax.rsqrt(jnp.mean(x * x, axis=-1, keepdims=True) + EPS) * w


def _silu(x):
    h = 0.5 * x
    return h + h * jnp.tanh(h)


def _softplus(x):
    return jnp.maximum(x, 0.0) + jnp.log1p(jnp.exp(-jnp.abs(x)))


def _lane_half(shape):
    return lax.broadcasted_iota(jnp.int32, shape, len(shape) - 1) % LANES < (LANES // 2)


def _mod_kernel(c_ref, w_ref, b_ref, o_ref):
    s = _silu(c_ref[...]).astype(BF16)
    o_ref[0] = _dot(s, w_ref[0].astype(BF16)) + b_ref[0]


def _modulation(cond, w_ada, b_ada):
    rows = cond.shape[0]
    n = w_ada.shape[-1]
    tn = 1536
    return pl.pallas_call(
        _mod_kernel,
        grid=(DEPTH, n // tn),
        in_specs=[
            pl.BlockSpec((rows, D_MODEL), lambda l, j: (0, 0)),
            pl.BlockSpec((1, D_MODEL, tn), lambda l, j: (l, 0, j)),
            pl.BlockSpec((1, 1, tn), lambda l, j: (l, 0, j)),
        ],
        out_specs=pl.BlockSpec((1, rows, tn), lambda l, j: (l, 0, j)),
        out_shape=jax.ShapeDtypeStruct((DEPTH, rows, n), F32),
        compiler_params=pltpu.CompilerParams(
            dimension_semantics=("arbitrary", "arbitrary"), vmem_limit_bytes=VMEM_LIMIT),
        name="modulation",
    )(cond, w_ada, b_ada.reshape(DEPTH, 1, n))


def _mod_spec(layer, per_row_tiles):
    base = layer * N_COND
    if per_row_tiles is None:
        index = lambda i, *_: (base, 0, 0)
    else:
        index = lambda i, *_: (base + 1 + i // per_row_tiles, 0, 0)
    return pl.BlockSpec((1, 1, 6 * D_MODEL), index)


def _token_tile(seq_len, shared_cond):
    return TOKEN_TILE if shared_cond else min(TOKEN_TILE, seq_len)


def _project_in(x_ref, mod_ref, n1w_ref, w_ref, u_s):
    sh1 = mod_ref[0, :, 0:D_MODEL]
    sc1 = mod_ref[0, :, D_MODEL:2 * D_MODEL]
    for r in range(0, x_ref.shape[0], PROJ_ROWS):
        h = (_rms(x_ref[r:r + PROJ_ROWS, :], n1w_ref[...]) * (1.0 + sc1) + sh1).astype(BF16)
        u_s[r:r + PROJ_ROWS, :] = _dot(h, w_ref[...])
        yield


def _interleave(gens):
    gens = list(gens)
    while gens:
        for g in list(gens):
            try:
                next(g)
            except StopIteration:
                gens.remove(g)


def _ssd_kernel(*refs, seq_len, has_h0, has_prev, group):
    if has_h0:
        hout_ref = None
        (x_ref, mod_ref, n1w_ref, w_ref, h0_ref, cw_ref, cb_ref, dtb_ref, alog_ref, dsk_ref, nw_ref,
         y_ref, *scratch) = refs
    else:
        h0_ref = None
        refs = refs[:10] + refs[10 + has_prev:]
        (x_ref, mod_ref, n1w_ref, w_ref, cw_ref, cb_ref, dtb_ref, alog_ref, dsk_ref, nw_ref,
         y_ref, hout_ref, *scratch) = refs
    _interleave(
        _ssd_seq(x_ref.at[s], mod_ref, n1w_ref, w_ref, None if h0_ref is None else h0_ref.at[s, 0],
                 cw_ref, cb_ref, dtb_ref, alog_ref, dsk_ref, nw_ref, y_ref.at[s],
                 None if hout_ref is None else hout_ref.at[s], *[r.at[s] for r in scratch], seq_len=seq_len)
        for s in range(group))


def _ssd_seq(x_ref, mod_ref, n1w_ref, w_ref, h0_ref, cw_ref, cb_ref, dtb_ref, alog_ref, dsk_ref, nw_ref,
             y_ref, hout_ref, u_s, xpad_s, xbc_s, dt_s, la_s, y_s, st_s, *, seq_len):
    has_h0 = h0_ref is not None
    L = seq_len
    nc = L // CHUNK
    yield from _project_in(x_ref, mod_ref, n1w_ref, w_ref, u_s)

    xpad_s[0:HALO, :] = jnp.zeros((HALO, CONV_DIM), BF16)
    xpad_s[HALO + L:2 * HALO + L, :] = jnp.zeros((HALO, CONV_DIM), BF16)
    xpad_s[HALO:HALO + L, :] = u_s[:, D_SSD:D_SSD + CONV_DIM].astype(BF16)
    side_taps = [k for k in range(CONV_W) if k != CONV_W // 2]
    sr = lax.broadcasted_iota(jnp.int32, (len(side_taps) * CHUNK, 2 * CHUNK), 0)
    sc = lax.broadcasted_iota(jnp.int32, (len(side_taps) * CHUNK, 2 * CHUNK), 1)
    src_row = sr % CHUNK + HALO
    for i, k in enumerate(side_taps):
        src_row = src_row + jnp.where(sr // CHUNK == i, k - CONV_W // 2, 0)
    shift = jnp.where(sc == src_row, 1.0, 0.0).astype(BF16)

    for c in range(nc):
        base = c * CHUNK
        moved = _dot(shift, xpad_s[base:base + 2 * CHUNK, :])
        centre = u_s[base:base + CHUNK, D_SSD:D_SSD + CONV_DIM]
        acc = cb_ref[...] + centre * cw_ref[CONV_W // 2:CONV_W // 2 + 1, :]
        for i, k in enumerate(side_taps):
            acc = acc + moved[i * CHUNK:(i + 1) * CHUNK, :] * cw_ref[k:k + 1, :]
        xbc_s[base:base + CHUNK, :] = _silu(acc)
        yield

    dt = _softplus(u_s[:, D_SSD + CONV_DIM:U_SSD] + dtb_ref[...])
    dt_s[...] = dt
    la_s[...] = dt * (-jnp.exp(alog_ref[...]))
    y_s[...] = xbc_s[:, 0:D_SSD] * dsk_ref[...]

    if has_h0:
        for d in range(2):
            for h in range(H_SSD):
                g, hh = divmod(h, H_SSD // G_SSD)
                st_s[d, g * N_SSD:(g + 1) * N_SSD, hh * P_SSD:(hh + 1) * P_SSD] = h0_ref[d, h]
    else:
        st_s[...] = jnp.zeros(st_s.shape, F32)

    ri = lax.broadcasted_iota(jnp.int32, (CHUNK, CHUNK), 0)
    ci = lax.broadcasted_iota(jnp.int32, (CHUNK, CHUNK), 1)
    first_group_lanes = _lane_half((CHUNK, LANES))
    first_group_rows = lax.broadcasted_iota(jnp.int32, (2 * N_SSD, 4 * P_SSD), 0) < N_SSD
    half_row = _lane_half((1, LANES))
    er = lax.broadcasted_iota(jnp.int32, (LANES, D_SSD), 0)
    el = lax.broadcasted_iota(jnp.int32, (LANES, D_SSD), 1) // P_SSD
    expand = [jnp.where(er == d * H_SSD + el, 1.0, 0.0).astype(BF16) for d in range(2)]

    keeps = [ri >= ci, ri <= ci]
    tris = [jnp.where(kp, 1.0, 0.0).astype(BF16) for kp in keeps]

    def stage_a(start, d):
        rows = pl.ds(start, CHUNK)
        cs = _dot_exact_rhs(tris[d], la_s[rows, :])
        tot = cs[CHUNK - 1:CHUNK, :] if d == 0 else cs[0:1, :]
        dt_wide = _dot_exact_lhs(dt_s[rows, :], expand[d])
        tot_wide = _dot_exact_lhs(jnp.broadcast_to(tot, (SUBLANES, LANES)), expand[d])[0:1, :]
        bmat = xbc_s[rows, D_SSD:D_SSD + LANES]
        cmat = xbc_s[rows, D_SSD + LANES:D_SSD + 2 * LANES]
        b_bf = bmat.astype(BF16)
        c_grp = [jnp.where(first_group_lanes, cmat, 0.0).astype(BF16),
                 jnp.where(first_group_lanes, 0.0, cmat).astype(BF16)]
        gmat = [_dot_nt(c_grp[g], b_bf) for g in range(G_SSD)]
        st_bf = st_s[d].astype(BF16)
        y_inter = [_dot(c_grp[g], st_bf) for g in range(G_SSD)]
        return dict(rows=rows, d=d, cs=cs, cst=cs.T, tot=tot, dt_wide=dt_wide, tot_wide=tot_wide,
                    gmat=gmat, y_inter=y_inter, bt=bmat.T.astype(BF16))

    def stage_b(t):
        rows, d, cs, cst, tot, dt_wide = t["rows"], t["d"], t["cs"], t["cst"], t["tot"], t["dt_wide"]
        gmat, y_inter, keep = t["gmat"], t["y_inter"], keeps[t["d"]]
        vte = []
        for pr in range(H_SSD // 2):
            g = pr // 2
            ia = d * H_SSD + 2 * pr
            ib = ia + 1
            pair = slice(pr * LANES, (pr + 1) * LANES)
            v = xbc_s[rows, pair] * dt_wide[:, pair]
            va = jnp.where(first_group_lanes, v, 0.0).astype(BF16)
            vb = jnp.where(first_group_lanes, 0.0, v).astype(BF16)
            col_a = jnp.broadcast_to(cs[:, ia:ia + 1], (CHUNK, LANES))
            col_b = jnp.broadcast_to(cs[:, ib:ib + 1], (CHUNK, LANES))
            seg_a = col_a - cst[ia:ia + 1, :]
            seg_b = col_b - cst[ib:ib + 1, :]
            s_a = (gmat[g] * jnp.exp(jnp.where(keep, seg_a, -jnp.inf))).astype(BF16)
            s_b = (gmat[g] * jnp.exp(jnp.where(keep, seg_b, -jnp.inf))).astype(BF16)
            y_intra = _dot(s_a, va) + _dot(s_b, vb)
            col_pair = jnp.where(first_group_lanes, col_a, col_b)
            tot_pair = jnp.where(half_row, tot[:, ia:ia + 1], tot[:, ib:ib + 1])
            lo = (pr % 2) * LANES
            y_pair = y_intra + y_inter[g][:, lo:lo + LANES] * jnp.exp(col_pair)
            y_s[rows, pair] += y_pair
            vte.append((v * jnp.exp(tot_pair - col_pair)).astype(BF16))

        bt = t["bt"]
        new0 = _dot(bt, jnp.concatenate([vte[0], vte[1]], axis=1))
        new1 = _dot(bt, jnp.concatenate([vte[2], vte[3]], axis=1))
        new = jnp.where(first_group_rows, new0, new1)
        cd = jnp.exp(t["tot_wide"])
        half_w = 4 * P_SSD
        decay = jnp.where(first_group_rows, cd[:, 0:half_w], cd[:, half_w:2 * half_w])
        st_s[d] = st_s[d] * decay + new

    tasks = [((c if d == 0 else nc - 1 - c) * CHUNK, d) for c in range(nc) for d in range(2)]
    yield
    pending = stage_a(*tasks[0])
    for nxt in tasks[1:]:
        yield
        upcoming = stage_a(*nxt)
        yield
        stage_b(pending)
        pending = upcoming
    yield
    stage_b(pending)
    yield

    if hout_ref is not None:
        for d in range(2):
            for h in range(H_SSD):
                g, hh = divmod(h, H_SSD // G_SSD)
                hout_ref[d, h] = st_s[d, g * N_SSD:(g + 1) * N_SSD, hh * P_SSD:(hh + 1) * P_SSD]

    y = y_s[...] * _silu(u_s[:, 0:D_SSD])
    y_ref[...] = _rms(y, nw_ref[...])


def _layer_spec(layer, *tail):
    zeros = (0,) * len(tail)
    return pl.BlockSpec((None,) + tail, lambda *_: (layer,) + zeros)


def _state_out(nb, layer, prev, *tail, group=1):
    zeros = (0,) * len(tail)
    spec = pl.BlockSpec((group, None) + tail, lambda b: (b, layer) + zeros)
    shape = jax.ShapeDtypeStruct((nb, DEPTH) + tail, F32)
    return spec, shape, ([] if prev is None else [prev])


def _seq_group(seq_len, shared_cond):
    return max(1, GROUP_TOKENS // seq_len) if shared_cond else 1


def _seq_inputs(x, mod, seq_len, shared_cond, layer, n1w, w, width, group=1):
    nb = x.shape[0] // seq_len
    assert nb % group == 0 and (shared_cond or group == 1)
    w_all, w_col = w
    specs = [pl.BlockSpec((group, seq_len, D_MODEL), lambda b: (b, 0, 0)),
             _mod_spec(layer, None if shared_cond else 1),
             _layer_spec(layer, 1, D_MODEL),
             pl.BlockSpec((None, D_MODEL, width), lambda b: (layer, 0, w_col))]
    return nb, specs, [x.reshape(nb, seq_len, D_MODEL), mod, n1w, w_all]


def _ssd(x, mod, seq_len, shared_cond, h0, layer, prev_states, n1w, w, cw, cb, dtb, alog, dsk, nw):
    grp = _seq_group(seq_len, shared_cond)
    nb, in_specs, args = _seq_inputs(x, mod, seq_len, shared_cond, layer, n1w, w, U_SSD, grp)
    has_h0 = h0 is not None
    if has_h0:
        in_specs.append(pl.BlockSpec((grp, 1, 2, H_SSD, N_SSD, P_SSD), lambda b: (b, layer, 0, 0, 0, 0)))
        args.append(h0)
    in_specs += [
        _layer_spec(layer, 8, CONV_DIM),
        _layer_spec(layer, 1, CONV_DIM),
        _layer_spec(layer, 1, LANES),
        _layer_spec(layer, 1, LANES),
        _layer_spec(layer, 1, D_SSD),
        _layer_spec(layer, 1, D_SSD),
    ]
    args += [cw, cb, dtb, alog, dsk, nw]
    out_specs = [pl.BlockSpec((grp, seq_len, D_SSD), lambda b: (b, 0, 0))]
    out_shape = [jax.ShapeDtypeStruct((nb, seq_len, D_SSD), F32)]
    aliases = {}
    if not has_h0:
        spec, shape, prev = _state_out(nb, layer, prev_states, 2, H_SSD, N_SSD, P_SSD, group=grp)
        out_specs.append(spec)
        out_shape.append(shape)
        if prev:
            aliases = {len(args): 1}
            in_specs.append(pl.BlockSpec(memory_space=pl.ANY))
            args += prev
    outs = pl.pallas_call(
        functools.partial(_ssd_kernel, seq_len=seq_len, has_h0=has_h0, has_prev=bool(aliases), group=grp),
        grid=(nb // grp,),
        in_specs=in_specs,
        out_specs=out_specs,
        out_shape=out_shape,
        input_output_aliases=aliases,
        scratch_shapes=[
            pltpu.VMEM((grp, seq_len, U_SSD), F32),
            pltpu.VMEM((grp, seq_len + 2 * HALO, CONV_DIM), BF16),
            pltpu.VMEM((grp, seq_len, CONV_DIM), F32),
            pltpu.VMEM((grp, seq_len, LANES), F32),
            pltpu.VMEM((grp, seq_len, LANES), F32),
            pltpu.VMEM((grp, seq_len, D_SSD), F32),
            pltpu.VMEM((grp, 2, 2 * N_SSD, 4 * P_SSD), F32),
        ],
        compiler_params=pltpu.CompilerParams(
            dimension_semantics=("arbitrary",), vmem_limit_bytes=VMEM_LIMIT),
        name="ssd",
    )(*args)
    return outs[0].reshape(nb * seq_len, D_SSD), (None if has_h0 else outs[1])


def _ret_kernel(*refs, seq_len, has_h0, has_prev, group):
    if has_h0:
        hout_ref = None
        x_ref, mod_ref, n1w_ref, w_ref, h0_ref, dl_ref, gnw_ref, y_ref, *scratch = refs
    else:
        h0_ref = None
        refs = refs[:6] + refs[6 + has_prev:]
        x_ref, mod_ref, n1w_ref, w_ref, dl_ref, gnw_ref, y_ref, hout_ref, *scratch = refs
    _interleave(
        _ret_seq(x_ref.at[s], mod_ref, n1w_ref, w_ref, None if h0_ref is None else h0_ref.at[s, 0],
                 dl_ref, gnw_ref, y_ref.at[s], None if hout_ref is None else hout_ref.at[s],
                 *[r.at[s] for r in scratch], seq_len=seq_len)
        for s in range(group))


def _ret_seq(x_ref, mod_ref, n1w_ref, w_ref, h0_ref, dl_ref, gnw_ref, y_ref, hout_ref, u_s, y_s, st_s,
             *, seq_len):
    has_h0 = h0_ref is not None
    yield from _project_in(x_ref, mod_ref, n1w_ref, w_ref, u_s)
    L = seq_len
    nc = L // CHUNK
    npair = H_RET // 2
    nq = H_RET * DK_RET

    lg_row = -_softplus(-dl_ref[...])
    ri = lax.broadcasted_iota(jnp.int32, (CHUNK, CHUNK), 0)
    ci = lax.broadcasted_iota(jnp.int32, (CHUNK, CHUNK), 1)
    rif = ri.astype(F32)
    half = _lane_half((CHUNK, LANES))
    block_diag = (ri < DK_RET) == (ci < DV_RET)

    if has_h0:
        st_s[...] = jnp.zeros(st_s.shape, F32)
        for d in range(2):
            for h in range(H_RET):
                pr, hh = divmod(h, 2)
                st_s[d, pr, hh * DK_RET:(hh + 1) * DK_RET, hh * DV_RET:(hh + 1) * DV_RET] = h0_ref[d, h]
    else:
        st_s[...] = jnp.zeros(st_s.shape, F32)
    y_s[...] = jnp.zeros(y_s.shape, F32)

    consts = {}
    for d in range(2):
        keep = (ri >= ci) if d == 0 else (ri <= ci)
        dist = (ri - ci).astype(F32) if d == 0 else (ci - ri).astype(F32)
        for pr in range(npair):
            ia = d * H_RET + 2 * pr
            lg_a = lg_row[:, ia:ia + 1]
            lg_b = lg_row[:, ia + 1:ia + 2]
            lg_pair = jnp.where(half, lg_a, lg_b)
            if d == 0:
                in_scale = jnp.exp((rif + 1.0) * lg_pair)
                to_end = jnp.exp((CHUNK - 1.0 - rif) * lg_pair)
            else:
                in_scale = jnp.exp((CHUNK - rif) * lg_pair)
                to_end = jnp.exp(rif * lg_pair)
            consts[d, pr] = dict(
                dec_a=jnp.exp(jnp.where(keep, dist * lg_a, -jnp.inf)),
                dec_b=jnp.exp(jnp.where(keep, dist * lg_b, -jnp.inf)),
                in_scale=in_scale, to_end=to_end, cd=jnp.exp(float(CHUNK) * lg_pair))

    def stage_a(start, d, pr):
        cst = consts[d, pr]
        rows = pl.ds(start, CHUNK)
        lanes = slice(pr * LANES, (pr + 1) * LANES)
        q = u_s[rows, lanes]
        k = u_s[rows, nq + pr * LANES:nq + (pr + 1) * LANES] * (DK_RET ** -0.5)
        v = u_s[rows, 2 * nq + pr * LANES:2 * nq + (pr + 1) * LANES]
        k_bf = k.astype(BF16)
        qk_a = _dot_nt(jnp.where(half, q, 0.0).astype(BF16), k_bf)
        qk_b = _dot_nt(jnp.where(half, 0.0, q).astype(BF16), k_bf)
        st = st_s[d, pr]
        y_inter = _dot(q.astype(BF16), st.astype(BF16)) * cst["in_scale"]
        new = _dot(k.T.astype(BF16), (v * cst["to_end"]).astype(BF16))
        st_s[d, pr] = st * cst["cd"] + jnp.where(block_diag, new, 0.0)
        return dict(rows=rows, lanes=lanes, cst=cst, qk_a=qk_a, qk_b=qk_b, y_inter=y_inter,
                    va=jnp.where(half, v, 0.0).astype(BF16), vb=jnp.where(half, 0.0, v).astype(BF16))

    def stage_b(t):
        s_a = (t["qk_a"] * t["cst"]["dec_a"]).astype(BF16)
        s_b = (t["qk_b"] * t["cst"]["dec_b"]).astype(BF16)
        y_s[t["rows"], t["lanes"]] += _dot(s_a, t["va"]) + _dot(s_b, t["vb"]) + t["y_inter"]

    tasks = [((c if d == 0 else nc - 1 - c) * CHUNK, d, pr)
             for c in range(nc) for d in range(2) for pr in range(npair)]
    yield
    pending = stage_a(*tasks[0])
    for nxt in tasks[1:]:
        yield
        upcoming = stage_a(*nxt)
        yield
        stage_b(pending)
        pending = upcoming
    yield
    stage_b(pending)
    yield

    if hout_ref is not None:
        for d in range(2):
            for h in range(H_RET):
                pr, hh = divmod(h, 2)
                hout_ref[d, h] = st_s[d, pr, hh * DK_RET:(hh + 1) * DK_RET, hh * DV_RET:(hh + 1) * DV_RET]

    half_l = _lane_half((L, LANES))
    for pr in range(npair):
        lanes = slice(pr * LANES, (pr + 1) * LANES)
        o = y_s[:, lanes]
        inv = 1.0 / DV_RET
        sum_a = jnp.sum(jnp.where(half_l, o, 0.0), axis=-1, keepdims=True)
        sum_all = jnp.sum(o, axis=-1, keepdims=True)
        mu = jnp.where(half_l, sum_a, sum_all - sum_a) * inv
        c = o - mu
        c2 = c * c
        sq_a = jnp.sum(jnp.where(half_l, c2, 0.0), axis=-1, keepdims=True)
        sq_all = jnp.sum(c2, axis=-1, keepdims=True)
        var = jnp.where(half_l, sq_a, sq_all - sq_a) * inv
        on = c * lax.rsqrt(var + EPS) * gnw_ref[:, lanes]
        gate = u_s[:, 2 * nq + D_RET + pr * LANES:2 * nq + D_RET + (pr + 1) * LANES]
        y_ref[:, lanes] = _silu(gate) * on
        yield


def _ret(x, mod, seq_len, shared_cond, h0, layer, prev_states, n1w, w, dl, gnw):
    grp = _seq_group(seq_len, shared_cond)
    nb, in_specs, args = _seq_inputs(x, mod, seq_len, shared_cond, layer, n1w, w, RET_PROJ, grp)
    has_h0 = h0 is not None
    if has_h0:
        in_specs.append(pl.BlockSpec((grp, 1, 2, H_RET, DK_RET, DV_RET), lambda b: (b, layer, 0, 0, 0, 0)))
        args.append(h0)
    in_specs += [_layer_spec(layer, 1, LANES), _layer_spec(layer, 1, D_RET)]
    args += [dl, gnw]
    out_specs = [pl.BlockSpec((grp, seq_len, D_RET), lambda b: (b, 0, 0))]
    out_shape = [jax.ShapeDtypeStruct((nb, seq_len, D_RET), F32)]
    aliases = {}
    if not has_h0:
        spec, shape, prev = _state_out(nb, layer, prev_states, 2, H_RET, DK_RET, DV_RET, group=grp)
        out_specs.append(spec)
        out_shape.append(shape)
        if prev:
            aliases = {len(args): 1}
            in_specs.append(pl.BlockSpec(memory_space=pl.ANY))
            args += prev
    outs = pl.pallas_call(
        functools.partial(_ret_kernel, seq_len=seq_len, has_h0=has_h0, has_prev=bool(aliases), group=grp),
        grid=(nb // grp,),
        in_specs=in_specs,
        out_specs=out_specs,
        out_shape=out_shape,
        input_output_aliases=aliases,
        scratch_shapes=[
            pltpu.VMEM((grp, seq_len, RET_PROJ), F32),
            pltpu.VMEM((grp, seq_len, D_RET), F32),
            pltpu.VMEM((grp, 2, H_RET // 2, 2 * DK_RET, 2 * DV_RET), F32),
        ],
        compiler_params=pltpu.CompilerParams(
            dimension_semantics=("arbitrary",), vmem_limit_bytes=VMEM_LIMIT),
        name="retention",
    )(*args)
    return outs[0].reshape(nb * seq_len, D_RET), (None if has_h0 else outs[1])


def _rot_matrix():
    r = lax.broadcasted_iota(jnp.int32, (LANES, LANES), 0)
    l = lax.broadcasted_iota(jnp.int32, (LANES, LANES), 1)
    even = (l // 8) % 2 == 0
    rope = l < ROPE_DIM
    p = jnp.where(rope & even & (r == l + 8), -1.0, 0.0) + jnp.where(rope & ~even & (r == l - 8), 1.0, 0.0)
    return p.astype(BF16)


def _mla_kernel(*refs, seq_len, past_len, n_prev, group):
    if past_len:
        (x_ref, mod_ref, n1w_ref, w_ref, pckv_ref, pkr_ref, cos_ref, sin_ref, qnw_ref, kvnw_ref,
         wq_ref, wqr_ref, wk_ref, wv_ref, o_ref, *scratch) = refs
        ckv_ref = kr_ref = None
    else:
        refs = refs[:9] + refs[9 + n_prev:]
        (x_ref, mod_ref, n1w_ref, w_ref, qnw_ref, kvnw_ref, wq_ref, wk_ref, wv_ref,
         o_ref, ckv_ref, kr_ref, *scratch) = refs
        pckv_ref = pkr_ref = cos_ref = sin_ref = wqr_ref = None
    at = lambda ref, *idx: None if ref is None else ref.at[idx]
    _interleave(
        _mla_seq(x_ref.at[s], mod_ref, n1w_ref, w_ref, at(pckv_ref, s, 0), at(pkr_ref, s, 0), cos_ref, sin_ref,
                 qnw_ref, kvnw_ref, wq_ref, wqr_ref, wk_ref, wv_ref, o_ref.at[s], at(ckv_ref, s), at(kr_ref, s),
                 *[r.at[s] for r in scratch], seq_len=seq_len, past_len=past_len)
        for s in range(group))


def _mla_seq(x_ref, mod_ref, n1w_ref, w_ref, pckv_ref, pkr_ref, cos_ref, sin_ref, qnw_ref, kvnw_ref,
             wq_ref, wqr_ref, wk_ref, wv_ref, o_ref, ckv_ref, kr_ref, u_s, q_s, k_s, v_s, ckv_s, kr_s,
             *, seq_len, past_len):
    yield from _project_in(x_ref, mod_ref, n1w_ref, w_ref, u_s)
    L = seq_len
    S = past_len + L
    nqb = L // CHUNK
    qscale = (NOPE_DIM + ROPE_DIM) ** -0.5 * math.log2(math.e)

    q_c = _rms(u_s[:, 0:Q_RANK], qnw_ref[...]).astype(BF16)
    ckv = _rms(u_s[:, Q_RANK:Q_RANK + KV_RANK], kvnw_ref[...])
    kr = u_s[:, Q_RANK + KV_RANK:U_MLA]
    if past_len:
        cos = cos_ref[...]
        sin = sin_ref[...]
        ckv_s[0:past_len, :] = pckv_ref[...]
        kr_s[0:past_len, :] = jnp.zeros((past_len, LANES), F32)
        kr_s[0:past_len, 0:ROPE_DIM] = pkr_ref[...]
        kr_s[past_len:S, :] = kr * cos + _dot_exact_lhs(kr, _rot_matrix()) * sin
    else:
        ckv_ref[...] = ckv
        kr_ref[...] = kr[:, 0:ROPE_DIM]
        kr_s[...] = kr
    ckv_s[past_len:S, :] = ckv

    ckv_all = ckv_s[...].astype(BF16)
    kr_all = kr_s[...]
    half_s = _lane_half((S, LANES))
    q_all = _dot(q_c, wq_ref[...])
    k_all = _dot(ckv_all, wk_ref[...])
    v_all = _dot(ckv_all, wv_ref[...])
    if past_len:
        qr_all = _dot(q_c, wqr_ref[...])
    for h in range(H_MLA):
        slab = slice(h * LANES, (h + 1) * LANES)
        qh = q_all[:, slab]
        if past_len:
            qh = qh * cos + qr_all[:, slab] * sin
        q_s[h] = (qh * qscale).astype(BF16)
        k_s[h] = (k_all[:, slab] + kr_all).astype(BF16)
    for pr in range(H_MLA // 2):
        vp = v_all[:, pr * LANES:(pr + 1) * LANES]
        v_s[2 * pr] = jnp.where(half_s, vp, 0.0).astype(BF16)
        v_s[2 * pr + 1] = jnp.where(half_s, 0.0, vp).astype(BF16)

    def scores(t):
        qb, h = divmod(t, H_MLA)
        return _dot_nt(q_s[h, qb * CHUNK:(qb + 1) * CHUNK, :], k_s[h])

    ntask = nqb * H_MLA
    yield
    s_next = scores(0)
    first = None
    for t in range(ntask):
        qb, h = divmod(t, H_MLA)
        s = s_next
        if t + 1 < ntask:
            s_next = scores(t + 1)
        e = jnp.exp2(s - jnp.max(s, axis=-1, keepdims=True))
        inv = 1.0 / jnp.sum(e, axis=-1, keepdims=True)
        o = _dot(e.astype(BF16), v_s[h]) * inv
        if h % 2 == 0:
            first = o
        else:
            pr = h // 2
            o_ref[qb * CHUNK:(qb + 1) * CHUNK, pr * LANES:(pr + 1) * LANES] = first + o
        yield


def _mla(x, mod, seq_len, shared_cond, cache, layer, rope, prev_cache, n1w, w, qnw, kvnw, wq, wqr, wk, wv):
    grp = _seq_group(seq_len, shared_cond)
    nb, in_specs, args = _seq_inputs(x, mod, seq_len, shared_cond, layer, n1w, w, U_MLA, grp)
    past_len = cache[0].shape[2] if cache is not None else 0
    s_len = past_len + seq_len
    full2 = lambda b: (0, 0)
    if past_len:
        in_specs += [
            pl.BlockSpec((grp, 1, past_len, KV_RANK), lambda b: (b, layer, 0, 0)),
            pl.BlockSpec((grp, 1, past_len, ROPE_DIM), lambda b: (b, layer, 0, 0)),
            pl.BlockSpec((seq_len, LANES), full2),
            pl.BlockSpec((seq_len, LANES), full2),
        ]
        args += [cache[0], cache[1], rope[0], rope[1]]
    in_specs += [_layer_spec(layer, 1, Q_RANK), _layer_spec(layer, 1, KV_RANK),
                 _layer_spec(layer, Q_RANK, H_MLA * LANES)]
    args += [qnw, kvnw, wq]
    if past_len:
        in_specs.append(_layer_spec(layer, Q_RANK, H_MLA * LANES))
        args.append(wqr)
    in_specs += [_layer_spec(layer, KV_RANK, H_MLA * LANES), _layer_spec(layer, KV_RANK, D_MLA)]
    args += [wk, wv]
    out_specs = [pl.BlockSpec((grp, seq_len, D_MLA), lambda b: (b, 0, 0))]
    out_shape = [jax.ShapeDtypeStruct((nb, seq_len, D_MLA), F32)]
    aliases = {}
    if not past_len:
        for i, width in enumerate((KV_RANK, ROPE_DIM)):
            spec, shape, prev = _state_out(nb, layer, None if prev_cache is None else prev_cache[i],
                                           seq_len, width, group=grp)
            out_specs.append(spec)
            out_shape.append(shape)
            if prev:
                aliases[len(args)] = 1 + i
                in_specs.append(pl.BlockSpec(memory_space=pl.ANY))
                args += prev
    outs = pl.pallas_call(
        functools.partial(_mla_kernel, seq_len=seq_len, past_len=past_len, n_prev=len(aliases), group=grp),
        grid=(nb // grp,),
        in_specs=in_specs,
        out_specs=out_specs,
        out_shape=out_shape,
        input_output_aliases=aliases,
        scratch_shapes=[
            pltpu.VMEM((grp, seq_len, U_MLA), F32),
            pltpu.VMEM((grp, H_MLA, seq_len, LANES), BF16),
            pltpu.VMEM((grp, H_MLA, s_len, LANES), BF16),
            pltpu.VMEM((grp, H_MLA, s_len, LANES), BF16),
            pltpu.VMEM((grp, s_len, KV_RANK), F32),
            pltpu.VMEM((grp, s_len, LANES), F32),
        ],
        compiler_params=pltpu.CompilerParams(
            dimension_semantics=("arbitrary",), vmem_limit_bytes=VMEM_LIMIT),
        name="mla",
    )(*args)
    o = outs[0].reshape(nb * seq_len, D_MLA)
    return (o,) + tuple(outs[1:])


def _out_ffn_kernel(x_ref, mssd_ref, mret_ref, mmla_ref, mod_ref, wos_ref, wor_ref, wom_ref, n2w_ref,
                    w1_ref, w2_ref, fnw_ref, o_ref, *, final):
    mix = (_dot(mssd_ref[...].astype(BF16), wos_ref[...])
           + _dot(mret_ref[...].astype(BF16), wor_ref[...])
           + _dot(mmla_ref[...].astype(BF16), wom_ref[...]))
    g1 = mod_ref[0, :, 2 * D_MODEL:3 * D_MODEL]
    sh2 = mod_ref[0, :, 3 * D_MODEL:4 * D_MODEL]
    sc2 = mod_ref[0, :, 4 * D_MODEL:5 * D_MODEL]
    g2 = mod_ref[0, :, 5 * D_MODEL:6 * D_MODEL]
    x1 = x_ref[...] + g1 * mix
    h2 = (_rms(x1, n2w_ref[...]) * (1.0 + sc2) + sh2).astype(BF16)

    def up(c):
        lo, hi = FF_SPLITS[c]
        return _dot(h2, w1_ref[:, lo:hi]), _dot(h2, w1_ref[:, D_FF + lo:D_FF + hi])

    acc = None
    nxt = up(0)
    for c, (lo, hi) in enumerate(FF_SPLITS):
        a, gt = nxt
        if c + 1 < len(FF_SPLITS):
            nxt = up(c + 1)
        down = _dot((_silu(a) * gt).astype(BF16), w2_ref[lo:hi, :])
        acc = down if acc is None else acc + down
    out = x1 + g2 * acc
    if final:
        out = _rms(out, fnw_ref[...])
    o_ref[...] = out


def _out_ffn(x, mssd, mret, mmla, mod, seq_len, shared_cond, layer, wo, n2w, w1, w2, fnw, final):
    t = x.shape[0]
    tm = _token_tile(seq_len, shared_cond)
    tok = lambda i: (i, 0)
    ret_blk = D_SSD // D_RET
    once = pl.Buffered(1)
    return pl.pallas_call(
        functools.partial(_out_ffn_kernel, final=final),
        grid=(t // tm,),
        in_specs=[
            pl.BlockSpec((tm, D_MODEL), tok),
            pl.BlockSpec((tm, D_SSD), tok),
            pl.BlockSpec((tm, D_RET), tok),
            pl.BlockSpec((tm, D_MLA), tok),
            _mod_spec(layer, None if shared_cond else seq_len // tm),
            pl.BlockSpec((None, D_SSD, D_MODEL), lambda i: (layer, 0, 0), pipeline_mode=once),
            pl.BlockSpec((None, D_RET, D_MODEL), lambda i: (layer, ret_blk, 0), pipeline_mode=once),
            pl.BlockSpec((None, D_MLA, D_MODEL), lambda i: (layer, ret_blk + 1, 0), pipeline_mode=once),
            _layer_spec(layer, 1, D_MODEL),
            pl.BlockSpec((None, D_MODEL, 2 * D_FF), lambda i: (layer, 0, 0), pipeline_mode=once),
            pl.BlockSpec((None, D_FF, D_MODEL), lambda i: (layer, 0, 0), pipeline_mode=once),
            pl.BlockSpec((1, D_MODEL), lambda i: (0, 0)),
        ],
        out_specs=pl.BlockSpec((tm, D_MODEL), tok),
        out_shape=jax.ShapeDtypeStruct((t, D_MODEL), F32),
        compiler_params=pltpu.CompilerParams(
            dimension_semantics=("arbitrary",), vmem_limit_bytes=VMEM_LIMIT),
        name="out_ffn",
    )(x, mssd, mret, mmla, mod, wo, wo, wo, n2w, w1, w2, fnw)


def _pad_lanes(a, width):
    return jnp.pad(a, [(0, 0)] * (a.ndim - 1) + [(0, width - a.shape[-1])])


def _rope_tables(n_tokens):
    n_rows = n_tokens // GRID_W
    row, col = jnp.meshgrid(jnp.arange(n_rows), jnp.arange(GRID_W), indexing='ij')
    row = row.reshape(-1).astype(F32)
    col = col.reshape(-1).astype(F32)
    half = ROPE_DIM // 2
    inv = ROPE_BASE ** (-jnp.arange(0, half, 2, dtype=F32) / half)
    ang_r = row[:, None] * inv
    ang_c = col[:, None] * inv
    ang = jnp.concatenate([ang_r, ang_r, ang_c, ang_c], axis=-1)
    cos = jnp.concatenate([jnp.cos(ang), jnp.ones((n_tokens, LANES - ROPE_DIM), F32)], axis=-1)
    sin = _pad_lanes(jnp.sin(ang), LANES)
    return cos, sin


def kernel(x_prompt, x_sample, c, state_ssd, state_ret, cache_mla_ckv, cache_mla_krope, c_ctx,
           w_ada, b_ada, norm1_w, w_in, ssd_conv_w, ssd_conv_b, ssd_dt_bias, ssd_A_log, ssd_D, ssd_norm_w,
           ret_decay_logit, ret_gn_w, mla_q_norm_w, mla_w_uq, mla_kv_norm_w, mla_w_ukv, w_out, norm2_w,
           ffn_w1, ffn_w2, final_norm_w):
    batch, seq, _ = x_prompt.shape
    dec_batch, dec_seq, _ = x_sample.shape

    ret0 = SSD_PROJ
    mla0 = SSD_PROJ + RET_PROJ
    ret_col, mla_col = 2, 6
    assert ret_col * RET_PROJ >= U_SSD and mla_col * U_MLA >= (ret_col + 1) * RET_PROJ
    w_cat = jnp.concatenate([
        _pad_lanes(w_in[..., 0:SSD_PROJ], ret_col * RET_PROJ),
        _pad_lanes(w_in[..., ret0:mla0], mla_col * U_MLA - ret_col * RET_PROJ),
        _pad_lanes(w_in[..., mla0:], U_MLA)], axis=-1).astype(BF16)
    w_ssd, w_ret, w_mla = (w_cat, 0), (w_cat, ret_col), (w_cat, mla_col)
    uq = mla_w_uq.reshape(DEPTH, Q_RANK, H_MLA, NOPE_DIM + ROPE_DIM).transpose(0, 2, 1, 3)
    def heads_side_by_side(w):
        return w.transpose(0, 2, 1, 3).reshape(DEPTH, w.shape[2], w.shape[1] * LANES).astype(BF16)

    wq = heads_side_by_side(_pad_lanes(jnp.concatenate([uq[..., NOPE_DIM:], uq[..., :NOPE_DIM]], axis=-1), LANES))
    qr = uq[..., NOPE_DIM:].reshape(DEPTH, H_MLA, Q_RANK, 2, 2, ROPE_DIM // 4)
    wq_rot = jnp.stack([-qr[..., 1, :], qr[..., 0, :]], axis=-2).reshape(DEPTH, H_MLA, Q_RANK, ROPE_DIM)
    wq_rot = heads_side_by_side(_pad_lanes(wq_rot, LANES))
    ukv = mla_w_ukv.reshape(DEPTH, KV_RANK, H_MLA, NOPE_DIM + V_DIM).transpose(0, 2, 1, 3)
    wk = heads_side_by_side(
        jnp.pad(ukv[..., :NOPE_DIM], [(0, 0)] * 3 + [(ROPE_DIM, LANES - ROPE_DIM - NOPE_DIM)]))
    wv = ukv[..., NOPE_DIM:].transpose(0, 2, 1, 3).reshape(DEPTH, KV_RANK, D_MLA).astype(BF16)
    wo = w_out.astype(BF16)
    w1 = ffn_w1.astype(BF16)
    w2 = ffn_w2.astype(BF16)
    conv_w = jnp.pad(ssd_conv_w, [(0, 0), (0, 8 - CONV_W), (0, 0)])
    conv_b = ssd_conv_b.reshape(DEPTH, 1, CONV_DIM)
    dt_bias = _pad_lanes(ssd_dt_bias.reshape(DEPTH, 1, 2 * H_SSD), LANES)
    a_log = _pad_lanes(ssd_A_log.reshape(DEPTH, 1, 2 * H_SSD), LANES)
    d_skip = jnp.repeat(ssd_D, P_SSD, axis=-1).reshape(DEPTH, 1, D_SSD)
    ssd_nw = ssd_norm_w.reshape(DEPTH, 1, D_SSD)
    decay_logit = _pad_lanes(ret_decay_logit.reshape(DEPTH, 1, 2 * H_RET), LANES)
    gn_w = ret_gn_w.reshape(DEPTH, 1, D_RET)
    q_nw = mla_q_norm_w.reshape(DEPTH, 1, Q_RANK)
    kv_nw = mla_kv_norm_w.reshape(DEPTH, 1, KV_RANK)
    n1w = norm1_w.reshape(DEPTH, 1, D_MODEL)
    n2w = norm2_w.reshape(DEPTH, 1, D_MODEL)
    fnw = final_norm_w.reshape(1, D_MODEL)
    rope = _rope_tables(dec_seq)

    assert dec_batch < N_COND
    cond = jnp.concatenate([c_ctx[None, :], c, jnp.zeros((N_COND - 1 - dec_batch, D_MODEL), F32)], axis=0)
    mod = _modulation(cond, w_ada, b_ada).reshape(DEPTH * N_COND, 1, 6 * D_MODEL)

    def layer(i, x, seq_len, shared_cond, h0_ssd, h0_ret, cache, prev):
        final = i == DEPTH - 1
        y_ssd, s_ssd = _ssd(x, mod, seq_len, shared_cond, h0_ssd, i, prev[0], n1w, w_ssd,
                            conv_w, conv_b, dt_bias, a_log, d_skip, ssd_nw)
        y_ret, s_ret = _ret(x, mod, seq_len, shared_cond, h0_ret, i, prev[1], n1w, w_ret, decay_logit, gn_w)
        mla_out = _mla(x, mod, seq_len, shared_cond, cache, i, rope, prev[2], n1w, w_mla,
                       q_nw, kv_nw, wq, wq_rot, wk, wv)
        x = _out_ffn(x, y_ssd, y_ret, mla_out[0], mod, seq_len, shared_cond, i, wo, n2w, w1, w2, fnw, final)
        return x, (s_ssd, s_ret, mla_out[1:])

    xp = x_prompt.reshape(batch * seq, D_MODEL)
    stacked = (jnp.zeros((batch, DEPTH, 2, H_SSD, N_SSD, P_SSD), F32),
               jnp.zeros((batch, DEPTH, 2, H_RET, DK_RET, DV_RET), F32),
               (jnp.zeros((batch, DEPTH, seq, KV_RANK), F32), jnp.zeros((batch, DEPTH, seq, ROPE_DIM), F32)))
    for i in range(DEPTH):
        xp, stacked = layer(i, xp, seq, True, None, None, None, stacked)

    xs = x_sample.reshape(dec_batch * dec_seq, D_MODEL)
    for i in range(DEPTH):
        xs, _ = layer(i, xs, dec_seq, False, state_ssd, state_ret, (cache_mla_ckv, cache_mla_krope),
                      (None, None, None))

    new_ssd, new_ret, (new_ckv, new_kr) = stacked
    return (xp.reshape(batch, seq, D_MODEL), xs.reshape(dec_batch, dec_seq, D_MODEL),
            new_ssd, new_ret, new_ckv, new_kr)
```

```python
import functools
import math

import jax
import jax.numpy as jnp
from jax import lax
from jax.experimental import pallas as pl
from jax.experimental.pallas import tpu as pltpu

F32 = jnp.float32
BF16 = jnp.bfloat16

D_MODEL = 1024
DEPTH = 4
GRID_W = 64
CHUNK = 128
LANES = 128
SUBLANES = 8
HALO = CHUNK // 2
EPS = 1e-6

H_SSD, P_SSD, N_SSD, G_SSD = 8, 64, 64, 2
D_SSD = H_SSD * P_SSD
CONV_W = 5
CONV_DIM = D_SSD + 2 * G_SSD * N_SSD
SSD_PROJ = D_SSD + CONV_DIM + 2 * H_SSD
U_SSD = D_SSD + CONV_DIM + LANES

H_RET, DK_RET, DV_RET = 4, 64, 64
D_RET = H_RET * DV_RET
RET_PROJ = 2 * H_RET * DK_RET + 2 * D_RET

H_MLA, Q_RANK, KV_RANK, NOPE_DIM, ROPE_DIM, V_DIM = 4, 256, 128, 64, 32, 64
D_MLA = H_MLA * V_DIM
U_MLA = Q_RANK + KV_RANK + LANES
ROPE_BASE = 10000.0

D_FF = 2816
FF_SPLITS = ((0, 1024), (1024, 2048), (2048, D_FF))
TOKEN_TILE = 512
PROJ_ROWS = 256
GROUP_TOKENS = 1024
N_COND = 16
VMEM_LIMIT = 56 * 1024 * 1024


def _dot(a, b):
    return jnp.dot(a, b, preferred_element_type=F32)


def _dot_nt(a, b):
    return lax.dot_general(a, b, (((1,), (1,)), ((), ())), preferred_element_type=F32)


def _split3(x):
    hi = x.astype(BF16)
    r1 = x - hi.astype(F32)
    mid = r1.astype(BF16)
    lo = (r1 - mid.astype(F32)).astype(BF16)
    return hi, mid, lo


def _dot_exact_rhs(ones_bf16, x):
    hi, mid, lo = _split3(x)
    return _dot(ones_bf16, hi) + _dot(ones_bf16, mid) + _dot(ones_bf16, lo)


def _dot_exact_lhs(x, ones_bf16):
    hi, mid, lo = _split3(x)
    return _dot(hi, ones_bf16) + _dot(mid, ones_bf16) + _dot(lo, ones_bf16)


def _rms(x, w):
    return x * lax.rsqrt(jnp.mean(x * x, axis=-1, keepdims=True) + EPS) * w


def _silu(x):
    h = 0.5 * x
    return h + h * jnp.tanh(h)


def _softplus(x):
    return jnp.maximum(x, 0.0) + jnp.log1p(jnp.exp(-jnp.abs(x)))


def _lane_half(shape):
    return lax.broadcasted_iota(jnp.int32, shape, len(shape) - 1) % LANES < (LANES // 2)


def _mod_kernel(c_ref, w_ref, b_ref, o_ref):
    s = _silu(c_ref[...]).astype(BF16)
    o_ref[0] = _dot(s, w_ref[0].astype(BF16)) + b_ref[0]


def _modulation(cond, w_ada, b_ada):
    rows = cond.shape[0]
    n = w_ada.shape[-1]
    tn = 1536
    return pl.pallas_call(
        _mod_kernel,
        grid=(DEPTH, n // tn),
        in_specs=[
            pl.BlockSpec((rows, D_MODEL), lambda l, j: (0, 0)),
            pl.BlockSpec((1, D_MODEL, tn), lambda l, j: (l, 0, j)),
            pl.BlockSpec((1, 1, tn), lambda l, j: (l, 0, j)),
        ],
        out_specs=pl.BlockSpec((1, rows, tn), lambda l, j: (l, 0, j)),
        out_shape=jax.ShapeDtypeStruct((DEPTH, rows, n), F32),
        compiler_params=pltpu.CompilerParams(
            dimension_semantics=("arbitrary", "arbitrary"), vmem_limit_bytes=VMEM_LIMIT),
        name="modulation",
    )(cond, w_ada, b_ada.reshape(DEPTH, 1, n))


def _mod_spec(layer, per_row_tiles):
    base = layer * N_COND
    if per_row_tiles is None:
        index = lambda i, *_: (base, 0, 0)
    else:
        index = lambda i, *_: (base + 1 + i // per_row_tiles, 0, 0)
    return pl.BlockSpec((1, 1, 6 * D_MODEL), index)


def _token_tile(seq_len, shared_cond):
    return TOKEN_TILE if shared_cond else min(TOKEN_TILE, seq_len)


def _project_in(x_ref, mod_ref, n1w_ref, w_ref, u_s):
    sh1 = mod_ref[0, :, 0:D_MODEL]
    sc1 = mod_ref[0, :, D_MODEL:2 * D_MODEL]
    for r in range(0, x_ref.shape[0], PROJ_ROWS):
        h = (_rms(x_ref[r:r + PROJ_ROWS, :], n1w_ref[...]) * (1.0 + sc1) + sh1).astype(BF16)
        u_s[r:r + PROJ_ROWS, :] = _dot(h, w_ref[...])
        yield


def _interleave(gens):
    gens = list(gens)
    while gens:
        for g in list(gens):
            try:
                next(g)
            except StopIteration:
                gens.remove(g)


def _ssd_kernel(*refs, seq_len, has_h0, has_prev, group):
    if has_h0:
        hout_ref = None
        (x_ref, mod_ref, n1w_ref, w_ref, h0_ref, cw_ref, cb_ref, dtb_ref, alog_ref, dsk_ref, nw_ref,
         y_ref, *scratch) = refs
    else:
        h0_ref = None
        refs = refs[:10] + refs[10 + has_prev:]
        (x_ref, mod_ref, n1w_ref, w_ref, cw_ref, cb_ref, dtb_ref, alog_ref, dsk_ref, nw_ref,
         y_ref, hout_ref, *scratch) = refs
        hout_ref = _own_block(hout_ref, not has_prev)
    _interleave(
        _ssd_seq(x_ref.at[s], mod_ref, n1w_ref, w_ref, None if h0_ref is None else h0_ref.at[s, 0],
                 cw_ref, cb_ref, dtb_ref, alog_ref, dsk_ref, nw_ref, y_ref.at[s],
                 None if hout_ref is None else hout_ref.at[s], *[r.at[s] for r in scratch], seq_len=seq_len)
        for s in range(group))


def _ssd_seq(x_ref, mod_ref, n1w_ref, w_ref, h0_ref, cw_ref, cb_ref, dtb_ref, alog_ref, dsk_ref, nw_ref,
             y_ref, hout_ref, u_s, xpad_s, xbc_s, dt_s, la_s, y_s, st_s, *, seq_len):
    has_h0 = h0_ref is not None
    L = seq_len
    nc = L // CHUNK
    yield from _project_in(x_ref, mod_ref, n1w_ref, w_ref, u_s)

    xpad_s[0:HALO, :] = jnp.zeros((HALO, CONV_DIM), BF16)
    xpad_s[HALO + L:2 * HALO + L, :] = jnp.zeros((HALO, CONV_DIM), BF16)
    xpad_s[HALO:HALO + L, :] = u_s[:, D_SSD:D_SSD + CONV_DIM].astype(BF16)
    side_taps = [k for k in range(CONV_W) if k != CONV_W // 2]
    sr = lax.broadcasted_iota(jnp.int32, (len(side_taps) * CHUNK, 2 * CHUNK), 0)
    sc = lax.broadcasted_iota(jnp.int32, (len(side_taps) * CHUNK, 2 * CHUNK), 1)
    src_row = sr % CHUNK + HALO
    for i, k in enumerate(side_taps):
        src_row = src_row + jnp.where(sr // CHUNK == i, k - CONV_W // 2, 0)
    shift = jnp.where(sc == src_row, 1.0, 0.0).astype(BF16)

    for c in range(nc):
        base = c * CHUNK
        moved = _dot(shift, xpad_s[base:base + 2 * CHUNK, :])
        centre = u_s[base:base + CHUNK, D_SSD:D_SSD + CONV_DIM]
        acc = cb_ref[...] + centre * cw_ref[CONV_W // 2:CONV_W // 2 + 1, :]
        for i, k in enumerate(side_taps):
            acc = acc + moved[i * CHUNK:(i + 1) * CHUNK, :] * cw_ref[k:k + 1, :]
        xbc_s[base:base + CHUNK, :] = _silu(acc)
        yield

    dt = _softplus(u_s[:, D_SSD + CONV_DIM:U_SSD] + dtb_ref[...])
    dt_s[...] = dt
    la_s[...] = dt * (-jnp.exp(alog_ref[...]))
    y_s[...] = xbc_s[:, 0:D_SSD] * dsk_ref[...]

    if has_h0:
        for d in range(2):
            for h in range(H_SSD):
                g, hh = divmod(h, H_SSD // G_SSD)
                st_s[d, g * N_SSD:(g + 1) * N_SSD, hh * P_SSD:(hh + 1) * P_SSD] = h0_ref[d, h]
    else:
        st_s[...] = jnp.zeros(st_s.shape, F32)

    ri = lax.broadcasted_iota(jnp.int32, (CHUNK, CHUNK), 0)
    ci = lax.broadcasted_iota(jnp.int32, (CHUNK, CHUNK), 1)
    first_group_lanes = _lane_half((CHUNK, LANES))
    first_group_rows = lax.broadcasted_iota(jnp.int32, (2 * N_SSD, 4 * P_SSD), 0) < N_SSD
    half_row = _lane_half((1, LANES))
    er = lax.broadcasted_iota(jnp.int32, (LANES, D_SSD), 0)
    el = lax.broadcasted_iota(jnp.int32, (LANES, D_SSD), 1) // P_SSD
    expand = [jnp.where(er == d * H_SSD + el, 1.0, 0.0).astype(BF16) for d in range(2)]

    keeps = [ri >= ci, ri <= ci]
    tris = [jnp.where(kp, 1.0, 0.0).astype(BF16) for kp in keeps]

    def stage_a(start, d):
        rows = pl.ds(start, CHUNK)
        cs = _dot_exact_rhs(tris[d], la_s[rows, :])
        tot = cs[CHUNK - 1:CHUNK, :] if d == 0 else cs[0:1, :]
        dt_wide = _dot_exact_lhs(dt_s[rows, :], expand[d])
        tot_wide = _dot_exact_lhs(jnp.broadcast_to(tot, (SUBLANES, LANES)), expand[d])[0:1, :]
        bmat = xbc_s[rows, D_SSD:D_SSD + LANES]
        cmat = xbc_s[rows, D_SSD + LANES:D_SSD + 2 * LANES]
        b_bf = bmat.astype(BF16)
        c_grp = [jnp.where(first_group_lanes, cmat, 0.0).astype(BF16),
                 jnp.where(first_group_lanes, 0.0, cmat).astype(BF16)]
        gmat = [_dot_nt(c_grp[g], b_bf) for g in range(G_SSD)]
        st_bf = st_s[d].astype(BF16)
        y_inter = [_dot(c_grp[g], st_bf) for g in range(G_SSD)]
        return dict(rows=rows, d=d, cs=cs, cst=cs.T, tot=tot, dt_wide=dt_wide, tot_wide=tot_wide,
                    gmat=gmat, y_inter=y_inter, bt=bmat.T.astype(BF16))

    def stage_b(t):
        rows, d, cs, cst, tot, dt_wide = t["rows"], t["d"], t["cs"], t["cst"], t["tot"], t["dt_wide"]
        gmat, y_inter, keep = t["gmat"], t["y_inter"], keeps[t["d"]]
        vte = []
        for pr in range(H_SSD // 2):
            g = pr // 2
            ia = d * H_SSD + 2 * pr
            ib = ia + 1
            pair = slice(pr * LANES, (pr + 1) * LANES)
            v = xbc_s[rows, pair] * dt_wide[:, pair]
            va = jnp.where(first_group_lanes, v, 0.0).astype(BF16)
            vb = jnp.where(first_group_lanes, 0.0, v).astype(BF16)
            col_a = jnp.broadcast_to(cs[:, ia:ia + 1], (CHUNK, LANES))
            col_b = jnp.broadcast_to(cs[:, ib:ib + 1], (CHUNK, LANES))
            seg_a = col_a - cst[ia:ia + 1, :]
            seg_b = col_b - cst[ib:ib + 1, :]
            s_a = (gmat[g] * jnp.exp(jnp.where(keep, seg_a, -jnp.inf))).astype(BF16)
            s_b = (gmat[g] * jnp.exp(jnp.where(keep, seg_b, -jnp.inf))).astype(BF16)
            y_intra = _dot(s_a, va) + _dot(s_b, vb)
            col_pair = jnp.where(first_group_lanes, col_a, col_b)
            tot_pair = jnp.where(half_row, tot[:, ia:ia + 1], tot[:, ib:ib + 1])
            lo = (pr % 2) * LANES
            y_pair = y_intra + y_inter[g][:, lo:lo + LANES] * jnp.exp(col_pair)
            y_s[rows, pair] += y_pair
            vte.append((v * jnp.exp(tot_pair - col_pair)).astype(BF16))

        bt = t["bt"]
        new0 = _dot(bt, jnp.concatenate([vte[0], vte[1]], axis=1))
        new1 = _dot(bt, jnp.concatenate([vte[2], vte[3]], axis=1))
        new = jnp.where(first_group_rows, new0, new1)
        cd = jnp.exp(t["tot_wide"])
        half_w = 4 * P_SSD
        decay = jnp.where(first_group_rows, cd[:, 0:half_w], cd[:, half_w:2 * half_w])
        st_s[d] = st_s[d] * decay + new

    tasks = [((c if d == 0 else nc - 1 - c) * CHUNK, d) for c in range(nc) for d in range(2)]
    yield
    pending = stage_a(*tasks[0])
    for nxt in tasks[1:]:
        yield
        upcoming = stage_a(*nxt)
        yield
        stage_b(pending)
        pending = upcoming
    yield
    stage_b(pending)
    yield

    if hout_ref is not None:
        for d in range(2):
            for h in range(H_SSD):
                g, hh = divmod(h, H_SSD // G_SSD)
                hout_ref[d, h] = st_s[d, g * N_SSD:(g + 1) * N_SSD, hh * P_SSD:(hh + 1) * P_SSD]

    y = y_s[...] * _silu(u_s[:, 0:D_SSD])
    y_ref[...] = _rms(y, nw_ref[...])


def _layer_spec(layer, *tail):
    zeros = (0,) * len(tail)
    return pl.BlockSpec((None,) + tail, lambda *_: (layer,) + zeros)


def _state_out(nb, layer, prev, *tail, group=1):
    zeros = (0,) * len(tail)
    if prev is None:
        assert layer == 0
        spec = pl.BlockSpec((group, DEPTH) + tail, lambda b: (b, 0) + zeros)
    else:
        spec = pl.BlockSpec((group, None) + tail, lambda b: (b, layer) + zeros)
    shape = jax.ShapeDtypeStruct((nb, DEPTH) + tail, F32)
    return spec, shape, ([] if prev is None else [prev])


def _own_block(out_ref, creates):
    if out_ref is None or not creates:
        return out_ref
    out_ref[:, 1:] = jnp.zeros((out_ref.shape[0], DEPTH - 1) + out_ref.shape[2:], out_ref.dtype)
    return out_ref.at[:, 0]


def _seq_group(seq_len, shared_cond):
    return max(1, GROUP_TOKENS // seq_len) if shared_cond else 1


def _seq_inputs(x, mod, seq_len, shared_cond, layer, n1w, w, width, group=1):
    nb = x.shape[0] // seq_len
    assert nb % group == 0 and (shared_cond or group == 1)
    w_all, w_col = w
    specs = [pl.BlockSpec((group, seq_len, D_MODEL), lambda b: (b, 0, 0)),
             _mod_spec(layer, None if shared_cond else 1),
             _layer_spec(layer, 1, D_MODEL),
             pl.BlockSpec((None, D_MODEL, width), lambda b: (layer, 0, w_col))]
    return nb, specs, [x.reshape(nb, seq_len, D_MODEL), mod, n1w, w_all]


def _ssd(x, mod, seq_len, shared_cond, h0, layer, prev_states, n1w, w, cw, cb, dtb, alog, dsk, nw):
    grp = _seq_group(seq_len, shared_cond)
    nb, in_specs, args = _seq_inputs(x, mod, seq_len, shared_cond, layer, n1w, w, U_SSD, grp)
    has_h0 = h0 is not None
    if has_h0:
        in_specs.append(pl.BlockSpec((grp, 1, 2, H_SSD, N_SSD, P_SSD), lambda b: (b, layer, 0, 0, 0, 0)))
        args.append(h0)
    in_specs += [
        _layer_spec(layer, 8, CONV_DIM),
        _layer_spec(layer, 1, CONV_DIM),
        _layer_spec(layer, 1, LANES),
        _layer_spec(layer, 1, LANES),
        _layer_spec(layer, 1, D_SSD),
        _layer_spec(layer, 1, D_SSD),
    ]
    args += [cw, cb, dtb, alog, dsk, nw]
    out_specs = [pl.BlockSpec((grp, seq_len, D_SSD), lambda b: (b, 0, 0))]
    out_shape = [jax.ShapeDtypeStruct((nb, seq_len, D_SSD), F32)]
    aliases = {}
    if not has_h0:
        spec, shape, prev = _state_out(nb, layer, prev_states, 2, H_SSD, N_SSD, P_SSD, group=grp)
        out_specs.append(spec)
        out_shape.append(shape)
        if prev:
            aliases = {len(args): 1}
            in_specs.append(pl.BlockSpec(memory_space=pl.ANY))
            args += prev
    outs = pl.pallas_call(
        functools.partial(_ssd_kernel, seq_len=seq_len, has_h0=has_h0, has_prev=bool(aliases), group=grp),
        grid=(nb // grp,),
        in_specs=in_specs,
        out_specs=out_specs,
        out_shape=out_shape,
        input_output_aliases=aliases,
        scratch_shapes=[
            pltpu.VMEM((grp, seq_len, U_SSD), F32),
            pltpu.VMEM((grp, seq_len + 2 * HALO, CONV_DIM), BF16),
            pltpu.VMEM((grp, seq_len, CONV_DIM), F32),
            pltpu.VMEM((grp, seq_len, LANES), F32),
            pltpu.VMEM((grp, seq_len, LANES), F32),
            pltpu.VMEM((grp, seq_len, D_SSD), F32),
            pltpu.VMEM((grp, 2, 2 * N_SSD, 4 * P_SSD), F32),
        ],
        compiler_params=pltpu.CompilerParams(
            dimension_semantics=("arbitrary",), vmem_limit_bytes=VMEM_LIMIT),
        name="ssd",
    )(*args)
    return outs[0].reshape(nb * seq_len, D_SSD), (None if has_h0 else outs[1])


def _ret_kernel(*refs, seq_len, has_h0, has_prev, group):
    if has_h0:
        hout_ref = None
        x_ref, mod_ref, n1w_ref, w_ref, h0_ref, dl_ref, gnw_ref, y_ref, *scratch = refs
    else:
        h0_ref = None
        refs = refs[:6] + refs[6 + has_prev:]
        x_ref, mod_ref, n1w_ref, w_ref, dl_ref, gnw_ref, y_ref, hout_ref, *scratch = refs
        hout_ref = _own_block(hout_ref, not has_prev)
    _interleave(
        _ret_seq(x_ref.at[s], mod_ref, n1w_ref, w_ref, None if h0_ref is None else h0_ref.at[s, 0],
                 dl_ref, gnw_ref, y_ref.at[s], None if hout_ref is None else hout_ref.at[s],
                 *[r.at[s] for r in scratch], seq_len=seq_len)
        for s in range(group))


def _ret_seq(x_ref, mod_ref, n1w_ref, w_ref, h0_ref, dl_ref, gnw_ref, y_ref, hout_ref, u_s, y_s, st_s,
             *, seq_len):
    has_h0 = h0_ref is not None
    yield from _project_in(x_ref, mod_ref, n1w_ref, w_ref, u_s)
    L = seq_len
    nc = L // CHUNK
    npair = H_RET // 2
    nq = H_RET * DK_RET

    lg_row = -_softplus(-dl_ref[...])
    ri = lax.broadcasted_iota(jnp.int32, (CHUNK, CHUNK), 0)
    ci = lax.broadcasted_iota(jnp.int32, (CHUNK, CHUNK), 1)
    rif = ri.astype(F32)
    half = _lane_half((CHUNK, LANES))
    block_diag = (ri < DK_RET) == (ci < DV_RET)

    if has_h0:
        st_s[...] = jnp.zeros(st_s.shape, F32)
        for d in range(2):
            for h in range(H_RET):
                pr, hh = divmod(h, 2)
                st_s[d, pr, hh * DK_RET:(hh + 1) * DK_RET, hh * DV_RET:(hh + 1) * DV_RET] = h0_ref[d, h]
    else:
        st_s[...] = jnp.zeros(st_s.shape, F32)
    y_s[...] = jnp.zeros(y_s.shape, F32)

    consts = {}
    for d in range(2):
        keep = (ri >= ci) if d == 0 else (ri <= ci)
        dist = (ri - ci).astype(F32) if d == 0 else (ci - ri).astype(F32)
        for pr in range(npair):
            ia = d * H_RET + 2 * pr
            lg_a = lg_row[:, ia:ia + 1]
            lg_b = lg_row[:, ia + 1:ia + 2]
            lg_pair = jnp.where(half, lg_a, lg_b)
            if d == 0:
                in_scale = jnp.exp((rif + 1.0) * lg_pair)
                to_end = jnp.exp((CHUNK - 1.0 - rif) * lg_pair)
            else:
                in_scale = jnp.exp((CHUNK - rif) * lg_pair)
                to_end = jnp.exp(rif * lg_pair)
            consts[d, pr] = dict(
                dec_a=jnp.exp(jnp.where(keep, dist * lg_a, -jnp.inf)),
                dec_b=jnp.exp(jnp.where(keep, dist * lg_b, -jnp.inf)),
                in_scale=in_scale, to_end=to_end, cd=jnp.exp(float(CHUNK) * lg_pair))

    def stage_a(start, d, pr):
        cst = consts[d, pr]
        rows = pl.ds(start, CHUNK)
        lanes = slice(pr * LANES, (pr + 1) * LANES)
        q = u_s[rows, lanes]
        k = u_s[rows, nq + pr * LANES:nq + (pr + 1) * LANES] * (DK_RET ** -0.5)
        v = u_s[rows, 2 * nq + pr * LANES:2 * nq + (pr + 1) * LANES]
        k_bf = k.astype(BF16)
        qk_a = _dot_nt(jnp.where(half, q, 0.0).astype(BF16), k_bf)
        qk_b = _dot_nt(jnp.where(half, 0.0, q).astype(BF16), k_bf)
        st = st_s[d, pr]
        y_inter = _dot(q.astype(BF16), st.astype(BF16)) * cst["in_scale"]
        new = _dot(k.T.astype(BF16), (v * cst["to_end"]).astype(BF16))
        st_s[d, pr] = st * cst["cd"] + jnp.where(block_diag, new, 0.0)
        return dict(rows=rows, lanes=lanes, cst=cst, qk_a=qk_a, qk_b=qk_b, y_inter=y_inter,
                    va=jnp.where(half, v, 0.0).astype(BF16), vb=jnp.where(half, 0.0, v).astype(BF16))

    def stage_b(t):
        s_a = (t["qk_a"] * t["cst"]["dec_a"]).astype(BF16)
        s_b = (t["qk_b"] * t["cst"]["dec_b"]).astype(BF16)
        y_s[t["rows"], t["lanes"]] += _dot(s_a, t["va"]) + _dot(s_b, t["vb"]) + t["y_inter"]

    tasks = [((c if d == 0 else nc - 1 - c) * CHUNK, d, pr)
             for c in range(nc) for d in range(2) for pr in range(npair)]
    yield
    pending = stage_a(*tasks[0])
    for nxt in tasks[1:]:
        yield
        upcoming = stage_a(*nxt)
        yield
        stage_b(pending)
        pending = upcoming
    yield
    stage_b(pending)
    yield

    if hout_ref is not None:
        for d in range(2):
            for h in range(H_RET):
                pr, hh = divmod(h, 2)
                hout_ref[d, h] = st_s[d, pr, hh * DK_RET:(hh + 1) * DK_RET, hh * DV_RET:(hh + 1) * DV_RET]

    half_l = _lane_half((L, LANES))
    for pr in range(npair):
        lanes = slice(pr * LANES, (pr + 1) * LANES)
        o = y_s[:, lanes]
        inv = 1.0 / DV_RET
        sum_a = jnp.sum(jnp.where(half_l, o, 0.0), axis=-1, keepdims=True)
        sum_all = jnp.sum(o, axis=-1, keepdims=True)
        mu = jnp.where(half_l, sum_a, sum_all - sum_a) * inv
        c = o - mu
        c2 = c * c
        sq_a = jnp.sum(jnp.where(half_l, c2, 0.0), axis=-1, keepdims=True)
        sq_all = jnp.sum(c2, axis=-1, keepdims=True)
        var = jnp.where(half_l, sq_a, sq_all - sq_a) * inv
        on = c * lax.rsqrt(var + EPS) * gnw_ref[:, lanes]
        gate = u_s[:, 2 * nq + D_RET + pr * LANES:2 * nq + D_RET + (pr + 1) * LANES]
        y_ref[:, lanes] = _silu(gate) * on
        yield


def _ret(x, mod, seq_len, shared_cond, h0, layer, prev_states, n1w, w, dl, gnw):
    grp = _seq_group(seq_len, shared_cond)
    nb, in_specs, args = _seq_inputs(x, mod, seq_len, shared_cond, layer, n1w, w, RET_PROJ, grp)
    has_h0 = h0 is not None
    if has_h0:
        in_specs.append(pl.BlockSpec((grp, 1, 2, H_RET, DK_RET, DV_RET), lambda b: (b, layer, 0, 0, 0, 0)))
        args.append(h0)
    in_specs += [_layer_spec(layer, 1, LANES), _layer_spec(layer, 1, D_RET)]
    args += [dl, gnw]
    out_specs = [pl.BlockSpec((grp, seq_len, D_RET), lambda b: (b, 0, 0))]
    out_shape = [jax.ShapeDtypeStruct((nb, seq_len, D_RET), F32)]
    aliases = {}
    if not has_h0:
        spec, shape, prev = _state_out(nb, layer, prev_states, 2, H_RET, DK_RET, DV_RET, group=grp)
        out_specs.append(spec)
        out_shape.append(shape)
        if prev:
            aliases = {len(args): 1}
            in_specs.append(pl.BlockSpec(memory_space=pl.ANY))
            args += prev
    outs = pl.pallas_call(
        functools.partial(_ret_kernel, seq_len=seq_len, has_h0=has_h0, has_prev=bool(aliases), group=grp),
        grid=(nb // grp,),
        in_specs=in_specs,
        out_specs=out_specs,
        out_shape=out_shape,
        input_output_aliases=aliases,
        scratch_shapes=[
            pltpu.VMEM((grp, seq_len, RET_PROJ), F32),
            pltpu.VMEM((grp, seq_len, D_RET), F32),
            pltpu.VMEM((grp, 2, H_RET // 2, 2 * DK_RET, 2 * DV_RET), F32),
        ],
        compiler_params=pltpu.CompilerParams(
            dimension_semantics=("arbitrary",), vmem_limit_bytes=VMEM_LIMIT),
        name="retention",
    )(*args)
    return outs[0].reshape(nb * seq_len, D_RET), (None if has_h0 else outs[1])


def _rot_matrix():
    r = lax.broadcasted_iota(jnp.int32, (LANES, LANES), 0)
    l = lax.broadcasted_iota(jnp.int32, (LANES, LANES), 1)
    even = (l // 8) % 2 == 0
    rope = l < ROPE_DIM
    p = jnp.where(rope & even & (r == l + 8), -1.0, 0.0) + jnp.where(rope & ~even & (r == l - 8), 1.0, 0.0)
    return p.astype(BF16)


def _mla_kernel(*refs, seq_len, past_len, n_prev, group):
    if past_len:
        (x_ref, mod_ref, n1w_ref, w_ref, pckv_ref, pkr_ref, cos_ref, sin_ref, qnw_ref, kvnw_ref,
         wq_ref, wqr_ref, wk_ref, wv_ref, o_ref, *scratch) = refs
        ckv_ref = kr_ref = None
    else:
        refs = refs[:9] + refs[9 + n_prev:]
        (x_ref, mod_ref, n1w_ref, w_ref, qnw_ref, kvnw_ref, wq_ref, wk_ref, wv_ref,
         o_ref, ckv_ref, kr_ref, *scratch) = refs
        pckv_ref = pkr_ref = cos_ref = sin_ref = wqr_ref = None
        ckv_ref = _own_block(ckv_ref, n_prev == 0)
        kr_ref = _own_block(kr_ref, n_prev == 0)
    at = lambda ref, *idx: None if ref is None else ref.at[idx]
    _interleave(
        _mla_seq(x_ref.at[s], mod_ref, n1w_ref, w_ref, at(pckv_ref, s, 0), at(pkr_ref, s, 0), cos_ref, sin_ref,
                 qnw_ref, kvnw_ref, wq_ref, wqr_ref, wk_ref, wv_ref, o_ref.at[s], at(ckv_ref, s), at(kr_ref, s),
                 *[r.at[s] for r in scratch], seq_len=seq_len, past_len=past_len)
        for s in range(group))


def _mla_seq(x_ref, mod_ref, n1w_ref, w_ref, pckv_ref, pkr_ref, cos_ref, sin_ref, qnw_ref, kvnw_ref,
             wq_ref, wqr_ref, wk_ref, wv_ref, o_ref, ckv_ref, kr_ref, u_s, q_s, k_s, v_s, ckv_s, kr_s,
             *, seq_len, past_len):
    yield from _project_in(x_ref, mod_ref, n1w_ref, w_ref, u_s)
    L = seq_len
    S = past_len + L
    nqb = L // CHUNK
    qscale = (NOPE_DIM + ROPE_DIM) ** -0.5 * math.log2(math.e)

    q_c = _rms(u_s[:, 0:Q_RANK], qnw_ref[...]).astype(BF16)
    ckv = _rms(u_s[:, Q_RANK:Q_RANK + KV_RANK], kvnw_ref[...])
    kr = u_s[:, Q_RANK + KV_RANK:U_MLA]
    if past_len:
        cos = cos_ref[...]
        sin = sin_ref[...]
        ckv_s[0:past_len, :] = pckv_ref[...]
        kr_s[0:past_len, :] = jnp.zeros((past_len, LANES), F32)
        kr_s[0:past_len, 0:ROPE_DIM] = pkr_ref[...]
        kr_s[past_len:S, :] = kr * cos + _dot_exact_lhs(kr, _rot_matrix()) * sin
    else:
        ckv_ref[...] = ckv
        kr_ref[...] = kr[:, 0:ROPE_DIM]
        kr_s[...] = kr
    ckv_s[past_len:S, :] = ckv

    ckv_all = ckv_s[...].astype(BF16)
    kr_all = kr_s[...]
    half_s = _lane_half((S, LANES))
    q_all = _dot(q_c, wq_ref[...])
    k_all = _dot(ckv_all, wk_ref[...])
    v_all = _dot(ckv_all, wv_ref[...])
    if past_len:
        qr_all = _dot(q_c, wqr_ref[...])
    for h in range(H_MLA):
        slab = slice(h * LANES, (h + 1) * LANES)
        qh = q_all[:, slab]
        if past_len:
            qh = qh * cos + qr_all[:, slab] * sin
        q_s[h] = (qh * qscale).astype(BF16)
        k_s[h] = (k_all[:, slab] + kr_all).astype(BF16)
    for pr in range(H_MLA // 2):
        vp = v_all[:, pr * LANES:(pr + 1) * LANES]
        v_s[2 * pr] = jnp.where(half_s, vp, 0.0).astype(BF16)
        v_s[2 * pr + 1] = jnp.where(half_s, 0.0, vp).astype(BF16)

    def scores(t):
        qb, h = divmod(t, H_MLA)
        return _dot_nt(q_s[h, qb * CHUNK:(qb + 1) * CHUNK, :], k_s[h])

    ntask = nqb * H_MLA
    yield
    s_next = scores(0)
    first = None
    for t in range(ntask):
        qb, h = divmod(t, H_MLA)
        s = s_next
        if t + 1 < ntask:
            s_next = scores(t + 1)
        e = jnp.exp2(s - jnp.max(s, axis=-1, keepdims=True))
        inv = 1.0 / jnp.sum(e, axis=-1, keepdims=True)
        o = _dot(e.astype(BF16), v_s[h]) * inv
        if h % 2 == 0:
            first = o
        else:
            pr = h // 2
            o_ref[qb * CHUNK:(qb + 1) * CHUNK, pr * LANES:(pr + 1) * LANES] = first + o
        yield


def _mla(x, mod, seq_len, shared_cond, cache, layer, rope, prev_cache, n1w, w, qnw, kvnw, wq, wqr, wk, wv):
    grp = _seq_group(seq_len, shared_cond)
    nb, in_specs, args = _seq_inputs(x, mod, seq_len, shared_cond, layer, n1w, w, U_MLA, grp)
    past_len = cache[0].shape[2] if cache is not None else 0
    s_len = past_len + seq_len
    full2 = lambda b: (0, 0)
    if past_len:
        in_specs += [
            pl.BlockSpec((grp, 1, past_len, KV_RANK), lambda b: (b, layer, 0, 0)),
            pl.BlockSpec((grp, 1, past_len, ROPE_DIM), lambda b: (b, layer, 0, 0)),
            pl.BlockSpec((seq_len, LANES), full2),
            pl.BlockSpec((seq_len, LANES), full2),
        ]
        args += [cache[0], cache[1], rope[0], rope[1]]
    in_specs += [_layer_spec(layer, 1, Q_RANK), _layer_spec(layer, 1, KV_RANK),
                 _layer_spec(layer, Q_RANK, H_MLA * LANES)]
    args += [qnw, kvnw, wq]
    if past_len:
        in_specs.append(_layer_spec(layer, Q_RANK, H_MLA * LANES))
        args.append(wqr)
    in_specs += [_layer_spec(layer, KV_RANK, H_MLA * LANES), _layer_spec(layer, KV_RANK, D_MLA)]
    args += [wk, wv]
    out_specs = [pl.BlockSpec((grp, seq_len, D_MLA), lambda b: (b, 0, 0))]
    out_shape = [jax.ShapeDtypeStruct((nb, seq_len, D_MLA), F32)]
    aliases = {}
    if not past_len:
        for i, width in enumerate((KV_RANK, ROPE_DIM)):
            spec, shape, prev = _state_out(nb, layer, None if prev_cache is None else prev_cache[i],
                                           seq_len, width, group=grp)
            out_specs.append(spec)
            out_shape.append(shape)
            if prev:
                aliases[len(args)] = 1 + i
                in_specs.append(pl.BlockSpec(memory_space=pl.ANY))
                args += prev
    outs = pl.pallas_call(
        functools.partial(_mla_kernel, seq_len=seq_len, past_len=past_len, n_prev=len(aliases), group=grp),
        grid=(nb // grp,),
        in_specs=in_specs,
        out_specs=out_specs,
        out_shape=out_shape,
        input_output_aliases=aliases,
        scratch_shapes=[
            pltpu.VMEM((grp, seq_len, U_MLA), F32),
            pltpu.VMEM((grp, H_MLA, seq_len, LANES), BF16),
            pltpu.VMEM((grp, H_MLA, s_len, LANES), BF16),
            pltpu.VMEM((grp, H_MLA, s_len, LANES), BF16),
            pltpu.VMEM((grp, s_len, KV_RANK), F32),
            pltpu.VMEM((grp, s_len, LANES), F32),
        ],
        compiler_params=pltpu.CompilerParams(
            dimension_semantics=("arbitrary",), vmem_limit_bytes=VMEM_LIMIT),
        name="mla",
    )(*args)
    o = outs[0].reshape(nb * seq_len, D_MLA)
    return (o,) + tuple(outs[1:])


def _out_ffn_kernel(x_ref, mssd_ref, mret_ref, mmla_ref, mod_ref, wos_ref, wor_ref, wom_ref, n2w_ref,
                    w1_ref, w2_ref, fnw_ref, o_ref, *, final):
    mix = (_dot(mssd_ref[...].astype(BF16), wos_ref[...])
           + _dot(mret_ref[...].astype(BF16), wor_ref[...])
           + _dot(mmla_ref[...].astype(BF16), wom_ref[...]))
    g1 = mod_ref[0, :, 2 * D_MODEL:3 * D_MODEL]
    sh2 = mod_ref[0, :, 3 * D_MODEL:4 * D_MODEL]
    sc2 = mod_ref[0, :, 4 * D_MODEL:5 * D_MODEL]
    g2 = mod_ref[0, :, 5 * D_MODEL:6 * D_MODEL]
    x1 = x_ref[...] + g1 * mix
    h2 = (_rms(x1, n2w_ref[...]) * (1.0 + sc2) + sh2).astype(BF16)

    def up(c):
        lo, hi = FF_SPLITS[c]
        return _dot(h2, w1_ref[:, lo:hi]), _dot(h2, w1_ref[:, D_FF + lo:D_FF + hi])

    acc = None
    nxt = up(0)
    for c, (lo, hi) in enumerate(FF_SPLITS):
        a, gt = nxt
        if c + 1 < len(FF_SPLITS):
            nxt = up(c + 1)
        down = _dot((_silu(a) * gt).astype(BF16), w2_ref[lo:hi, :])
        acc = down if acc is None else acc + down
    out = x1 + g2 * acc
    if final:
        out = _rms(out, fnw_ref[...])
    o_ref[...] = out


def _out_ffn(x, mssd, mret, mmla, mod, seq_len, shared_cond, layer, wo, n2w, w1, w2, fnw, final):
    t = x.shape[0]
    tm = _token_tile(seq_len, shared_cond)
    tok = lambda i: (i, 0)
    ret_blk = D_SSD // D_RET
    once = pl.Buffered(1)
    return pl.pallas_call(
        functools.partial(_out_ffn_kernel, final=final),
        grid=(t // tm,),
        in_specs=[
            pl.BlockSpec((tm, D_MODEL), tok),
            pl.BlockSpec((tm, D_SSD), tok),
            pl.BlockSpec((tm, D_RET), tok),
            pl.BlockSpec((tm, D_MLA), tok),
            _mod_spec(layer, None if shared_cond else seq_len // tm),
            pl.BlockSpec((None, D_SSD, D_MODEL), lambda i: (layer, 0, 0), pipeline_mode=once),
            pl.BlockSpec((None, D_RET, D_MODEL), lambda i: (layer, ret_blk, 0), pipeline_mode=once),
            pl.BlockSpec((None, D_MLA, D_MODEL), lambda i: (layer, ret_blk + 1, 0), pipeline_mode=once),
            _layer_spec(layer, 1, D_MODEL),
            pl.BlockSpec((None, D_MODEL, 2 * D_FF), lambda i: (layer, 0, 0), pipeline_mode=once),
            pl.BlockSpec((None, D_FF, D_MODEL), lambda i: (layer, 0, 0), pipeline_mode=once),
            pl.BlockSpec((1, D_MODEL), lambda i: (0, 0)),
        ],
        out_specs=pl.BlockSpec((tm, D_MODEL), tok),
        out_shape=jax.ShapeDtypeStruct((t, D_MODEL), F32),
        compiler_params=pltpu.CompilerParams(
            dimension_semantics=("arbitrary",), vmem_limit_bytes=VMEM_LIMIT),
        name="out_ffn",
    )(x, mssd, mret, mmla, mod, wo, wo, wo, n2w, w1, w2, fnw)


def _pad_lanes(a, width):
    return jnp.pad(a, [(0, 0)] * (a.ndim - 1) + [(0, width - a.shape[-1])])


def _rope_tables(n_tokens):
    n_rows = n_tokens // GRID_W
    row, col = jnp.meshgrid(jnp.arange(n_rows), jnp.arange(GRID_W), indexing='ij')
    row = row.reshape(-1).astype(F32)
    col = col.reshape(-1).astype(F32)
    half = ROPE_DIM // 2
    inv = ROPE_BASE ** (-jnp.arange(0, half, 2, dtype=F32) / half)
    ang_r = row[:, None] * inv
    ang_c = col[:, None] * inv
    ang = jnp.concatenate([ang_r, ang_r, ang_c, ang_c], axis=-1)
    cos = jnp.concatenate([jnp.cos(ang), jnp.ones((n_tokens, LANES - ROPE_DIM), F32)], axis=-1)
    sin = _pad_lanes(jnp.sin(ang), LANES)
    return cos, sin


def kernel(x_prompt, x_sample, c, state_ssd, state_ret, cache_mla_ckv, cache_mla_krope, c_ctx,
           w_ada, b_ada, norm1_w, w_in, ssd_conv_w, ssd_conv_b, ssd_dt_bias, ssd_A_log, ssd_D, ssd_norm_w,
           ret_decay_logit, ret_gn_w, mla_q_norm_w, mla_w_uq, mla_kv_norm_w, mla_w_ukv, w_out, norm2_w,
           ffn_w1, ffn_w2, final_norm_w):
    batch, seq, _ = x_prompt.shape
    dec_batch, dec_seq, _ = x_sample.shape

    ret0 = SSD_PROJ
    mla0 = SSD_PROJ + RET_PROJ
    w_ssd = (_pad_lanes(w_in[..., 0:SSD_PROJ], U_SSD).astype(BF16), 0)
    w_ret = (w_in[..., ret0:mla0].astype(BF16), 0)
    w_mla = (_pad_lanes(w_in[..., mla0:], U_MLA).astype(BF16), 0)
    uq = mla_w_uq.reshape(DEPTH, Q_RANK, H_MLA, NOPE_DIM + ROPE_DIM).transpose(0, 2, 1, 3)
    def heads_side_by_side(w):
        return w.transpose(0, 2, 1, 3).reshape(DEPTH, w.shape[2], w.shape[1] * LANES).astype(BF16)

    wq = heads_side_by_side(_pad_lanes(jnp.concatenate([uq[..., NOPE_DIM:], uq[..., :NOPE_DIM]], axis=-1), LANES))
    qr = uq[..., NOPE_DIM:].reshape(DEPTH, H_MLA, Q_RANK, 2, 2, ROPE_DIM // 4)
    wq_rot = jnp.stack([-qr[..., 1, :], qr[..., 0, :]], axis=-2).reshape(DEPTH, H_MLA, Q_RANK, ROPE_DIM)
    wq_rot = heads_side_by_side(_pad_lanes(wq_rot, LANES))
    ukv = mla_w_ukv.reshape(DEPTH, KV_RANK, H_MLA, NOPE_DIM + V_DIM).transpose(0, 2, 1, 3)
    wk = heads_side_by_side(
        jnp.pad(ukv[..., :NOPE_DIM], [(0, 0)] * 3 + [(ROPE_DIM, LANES - ROPE_DIM - NOPE_DIM)]))
    wv = ukv[..., NOPE_DIM:].transpose(0, 2, 1, 3).reshape(DEPTH, KV_RANK, D_MLA).astype(BF16)
    wo = w_out.astype(BF16)
    w1 = ffn_w1.astype(BF16)
    w2 = ffn_w2.astype(BF16)
    conv_w = jnp.pad(ssd_conv_w, [(0, 0), (0, 8 - CONV_W), (0, 0)])
    conv_b = ssd_conv_b.reshape(DEPTH, 1, CONV_DIM)
    dt_bias = _pad_lanes(ssd_dt_bias.reshape(DEPTH, 1, 2 * H_SSD), LANES)
    a_log = _pad_lanes(ssd_A_log.reshape(DEPTH, 1, 2 * H_SSD), LANES)
    d_skip = jnp.repeat(ssd_D, P_SSD, axis=-1).reshape(DEPTH, 1, D_SSD)
    ssd_nw = ssd_norm_w.reshape(DEPTH, 1, D_SSD)
    decay_logit = _pad_lanes(ret_decay_logit.reshape(DEPTH, 1, 2 * H_RET), LANES)
    gn_w = ret_gn_w.reshape(DEPTH, 1, D_RET)
    q_nw = mla_q_norm_w.reshape(DEPTH, 1, Q_RANK)
    kv_nw = mla_kv_norm_w.reshape(DEPTH, 1, KV_RANK)
    n1w = norm1_w.reshape(DEPTH, 1, D_MODEL)
    n2w = norm2_w.reshape(DEPTH, 1, D_MODEL)
    fnw = final_norm_w.reshape(1, D_MODEL)
    rope = _rope_tables(dec_seq)

    assert dec_batch < N_COND
    cond = jnp.concatenate([c_ctx[None, :], c, jnp.zeros((N_COND - 1 - dec_batch, D_MODEL), F32)], axis=0)
    mod = _modulation(cond, w_ada, b_ada).reshape(DEPTH * N_COND, 1, 6 * D_MODEL)

    def layer(i, x, seq_len, shared_cond, h0_ssd, h0_ret, cache, prev):
        final = i == DEPTH - 1
        y_ssd, s_ssd = _ssd(x, mod, seq_len, shared_cond, h0_ssd, i, prev[0], n1w, w_ssd,
                            conv_w, conv_b, dt_bias, a_log, d_skip, ssd_nw)
        y_ret, s_ret = _ret(x, mod, seq_len, shared_cond, h0_ret, i, prev[1], n1w, w_ret, decay_logit, gn_w)
        mla_out = _mla(x, mod, seq_len, shared_cond, cache, i, rope, prev[2], n1w, w_mla,
                       q_nw, kv_nw, wq, wq_rot, wk, wv)
        x = _out_ffn(x, y_ssd, y_ret, mla_out[0], mod, seq_len, shared_cond, i, wo, n2w, w1, w2, fnw, final)
        return x, (s_ssd, s_ret, mla_out[1:])

    xp = x_prompt.reshape(batch * seq, D_MODEL)
    stacked = (None, None, None)
    for i in range(DEPTH):
        xp, stacked = layer(i, xp, seq, True, None, None, None, stacked)

    xs = x_sample.reshape(dec_batch * dec_seq, D_MODEL)
    for i in range(DEPTH):
        xs, _ = layer(i, xs, dec_seq, False, state_ssd, state_ret, (cache_mla_ckv, cache_mla_krope),
                      (None, None, None))

    new_ssd, new_ret, (new_ckv, new_kr) = stacked
    return (xp.reshape(batch, seq, D_MODEL), xs.reshape(dec_batch, dec_seq, D_MODEL),
            new_ssd, new_ret, new_ckv, new_kr)
```

```python
import functools
import math

import jax
import jax.numpy as jnp
from jax import lax
from jax.experimental import pallas as pl
from jax.experimental.pallas import tpu as pltpu

F32 = jnp.float32
BF16 = jnp.bfloat16

D_MODEL = 1024
DEPTH = 4
GRID_W = 64
CHUNK = 128
LANES = 128
SUBLANES = 8
HALO = CHUNK // 2
EPS = 1e-6

H_SSD, P_SSD, N_SSD, G_SSD = 8, 64, 64, 2
D_SSD = H_SSD * P_SSD
CONV_W = 5
CONV_DIM = D_SSD + 2 * G_SSD * N_SSD
SSD_PROJ = D_SSD + CONV_DIM + 2 * H_SSD
U_SSD = D_SSD + CONV_DIM + LANES

H_RET, DK_RET, DV_RET = 4, 64, 64
D_RET = H_RET * DV_RET
RET_PROJ = 2 * H_RET * DK_RET + 2 * D_RET

H_MLA, Q_RANK, KV_RANK, NOPE_DIM, ROPE_DIM, V_DIM = 4, 256, 128, 64, 32, 64
D_MLA = H_MLA * V_DIM
U_MLA = Q_RANK + KV_RANK + LANES
ROPE_BASE = 10000.0

D_FF = 2816
FF_SPLITS = ((0, 1024), (1024, 2048), (2048, D_FF))
TOKEN_TILE = 512
PROJ_ROWS = 256
GROUP_TOKENS = 1024
N_COND = 16
VMEM_LIMIT = 56 * 1024 * 1024


def _dot(a, b):
    return jnp.dot(a, b, preferred_element_type=F32)


def _dot_nt(a, b):
    return lax.dot_general(a, b, (((1,), (1,)), ((), ())), preferred_element_type=F32)


def _split3(x):
    hi = x.astype(BF16)
    r1 = x - hi.astype(F32)
    mid = r1.astype(BF16)
    lo = (r1 - mid.astype(F32)).astype(BF16)
    return hi, mid, lo


def _dot_exact_rhs(ones_bf16, x):
    hi, mid, lo = _split3(x)
    return _dot(ones_bf16, hi) + _dot(ones_bf16, mid) + _dot(ones_bf16, lo)


def _dot_exact_lhs(x, ones_bf16):
    hi, mid, lo = _split3(x)
    return _dot(hi, ones_bf16) + _dot(mid, ones_bf16) + _dot(lo, ones_bf16)


def _rms(x, w):
    return x * lax.rsqrt(jnp.mean(x * x, axis=-1, keepdims=True) + EPS) * w


def _silu(x):
    h = 0.5 * x
    return h + h * jnp.tanh(h)


def _softplus(x):
    return jnp.maximum(x, 0.0) + jnp.log1p(jnp.exp(-jnp.abs(x)))


def _lane_half(shape):
    return lax.broadcasted_iota(jnp.int32, shape, len(shape) - 1) % LANES < (LANES // 2)


def _mod_kernel(c_ref, w_ref, b_ref, o_ref):
    s = _silu(c_ref[...]).astype(BF16)
    o_ref[0] = _dot(s, w_ref[0].astype(BF16)) + b_ref[0]


def _modulation(cond, w_ada, b_ada):
    rows = cond.shape[0]
    n = w_ada.shape[-1]
    tn = 1536
    return pl.pallas_call(
        _mod_kernel,
        grid=(DEPTH, n // tn),
        in_specs=[
            pl.BlockSpec((rows, D_MODEL), lambda l, j: (0, 0)),
            pl.BlockSpec((1, D_MODEL, tn), lambda l, j: (l, 0, j)),
            pl.BlockSpec((1, 1, tn), lambda l, j: (l, 0, j)),
        ],
        out_specs=pl.BlockSpec((1, rows, tn), lambda l, j: (l, 0, j)),
        out_shape=jax.ShapeDtypeStruct((DEPTH, rows, n), F32),
        compiler_params=pltpu.CompilerParams(
            dimension_semantics=("arbitrary", "arbitrary"), vmem_limit_bytes=VMEM_LIMIT),
        name="modulation",
    )(cond, w_ada, b_ada.reshape(DEPTH, 1, n))


def _mod_spec(layer, per_row_tiles):
    base = layer * N_COND
    if per_row_tiles is None:
        index = lambda i, *_: (base, 0, 0)
    else:
        index = lambda i, *_: (base + 1 + i // per_row_tiles, 0, 0)
    return pl.BlockSpec((1, 1, 6 * D_MODEL), index)


def _token_tile(seq_len, shared_cond):
    return TOKEN_TILE if shared_cond else min(TOKEN_TILE, seq_len)


def _project_in(x_ref, mod_ref, n1w_ref, w_ref, u_s):
    sh1 = mod_ref[0, :, 0:D_MODEL]
    sc1 = mod_ref[0, :, D_MODEL:2 * D_MODEL]
    for r in range(0, x_ref.shape[0], PROJ_ROWS):
        h = (_rms(x_ref[r:r + PROJ_ROWS, :], n1w_ref[...]) * (1.0 + sc1) + sh1).astype(BF16)
        u_s[r:r + PROJ_ROWS, :] = _dot(h, w_ref[...])
        yield


def _interleave(gens):
    gens = list(gens)
    while gens:
        for g in list(gens):
            try:
                next(g)
            except StopIteration:
                gens.remove(g)


def _ssd_kernel(*refs, seq_len, has_h0, has_prev, group):
    if has_h0:
        hout_ref = None
        (x_ref, mod_ref, n1w_ref, w_ref, h0_ref, cw_ref, cb_ref, dtb_ref, alog_ref, dsk_ref, nw_ref,
         y_ref, *scratch) = refs
    else:
        h0_ref = None
        refs = refs[:10] + refs[10 + has_prev:]
        (x_ref, mod_ref, n1w_ref, w_ref, cw_ref, cb_ref, dtb_ref, alog_ref, dsk_ref, nw_ref,
         y_ref, hout_ref, *scratch) = refs
        hout_ref = _own_block(hout_ref, not has_prev)
    _interleave(
        _ssd_seq(x_ref.at[s], mod_ref, n1w_ref, w_ref, None if h0_ref is None else h0_ref.at[s, 0],
                 cw_ref, cb_ref, dtb_ref, alog_ref, dsk_ref, nw_ref, y_ref.at[s],
                 None if hout_ref is None else hout_ref.at[s], *[r.at[s] for r in scratch], seq_len=seq_len)
        for s in range(group))


def _ssd_seq(x_ref, mod_ref, n1w_ref, w_ref, h0_ref, cw_ref, cb_ref, dtb_ref, alog_ref, dsk_ref, nw_ref,
             y_ref, hout_ref, u_s, xpad_s, xbc_s, dt_s, la_s, y_s, st_s, *, seq_len):
    has_h0 = h0_ref is not None
    L = seq_len
    nc = L // CHUNK
    yield from _project_in(x_ref, mod_ref, n1w_ref, w_ref, u_s)

    xpad_s[0:HALO, :] = jnp.zeros((HALO, CONV_DIM), BF16)
    xpad_s[HALO + L:2 * HALO + L, :] = jnp.zeros((HALO, CONV_DIM), BF16)
    xpad_s[HALO:HALO + L, :] = u_s[:, D_SSD:D_SSD + CONV_DIM].astype(BF16)
    side_taps = [k for k in range(CONV_W) if k != CONV_W // 2]
    sr = lax.broadcasted_iota(jnp.int32, (len(side_taps) * CHUNK, 2 * CHUNK), 0)
    sc = lax.broadcasted_iota(jnp.int32, (len(side_taps) * CHUNK, 2 * CHUNK), 1)
    src_row = sr % CHUNK + HALO
    for i, k in enumerate(side_taps):
        src_row = src_row + jnp.where(sr // CHUNK == i, k - CONV_W // 2, 0)
    shift = jnp.where(sc == src_row, 1.0, 0.0).astype(BF16)

    for c in range(nc):
        base = c * CHUNK
        moved = _dot(shift, xpad_s[base:base + 2 * CHUNK, :])
        centre = u_s[base:base + CHUNK, D_SSD:D_SSD + CONV_DIM]
        acc = cb_ref[...] + centre * cw_ref[CONV_W // 2:CONV_W // 2 + 1, :]
        for i, k in enumerate(side_taps):
            acc = acc + moved[i * CHUNK:(i + 1) * CHUNK, :] * cw_ref[k:k + 1, :]
        xbc_s[base:base + CHUNK, :] = _silu(acc)
        yield

    dt = _softplus(u_s[:, D_SSD + CONV_DIM:U_SSD] + dtb_ref[...])
    dt_s[...] = dt
    la_s[...] = dt * (-jnp.exp(alog_ref[...]))
    y_s[...] = xbc_s[:, 0:D_SSD] * dsk_ref[...]

    if has_h0:
        for d in range(2):
            for h in range(H_SSD):
                g, hh = divmod(h, H_SSD // G_SSD)
                st_s[d, g * N_SSD:(g + 1) * N_SSD, hh * P_SSD:(hh + 1) * P_SSD] = h0_ref[d, h]
    else:
        st_s[...] = jnp.zeros(st_s.shape, F32)

    ri = lax.broadcasted_iota(jnp.int32, (CHUNK, CHUNK), 0)
    ci = lax.broadcasted_iota(jnp.int32, (CHUNK, CHUNK), 1)
    first_group_lanes = _lane_half((CHUNK, LANES))
    first_group_rows = lax.broadcasted_iota(jnp.int32, (2 * N_SSD, 4 * P_SSD), 0) < N_SSD
    half_row = _lane_half((1, LANES))
    er = lax.broadcasted_iota(jnp.int32, (LANES, D_SSD), 0)
    el = lax.broadcasted_iota(jnp.int32, (LANES, D_SSD), 1) // P_SSD
    expand = [jnp.where(er == d * H_SSD + el, 1.0, 0.0).astype(BF16) for d in range(2)]

    keeps = [ri >= ci, ri <= ci]
    tris = [jnp.where(kp, 1.0, 0.0).astype(BF16) for kp in keeps]

    def stage_a(start, d):
        rows = pl.ds(start, CHUNK)
        cs = _dot_exact_rhs(tris[d], la_s[rows, :])
        tot = cs[CHUNK - 1:CHUNK, :] if d == 0 else cs[0:1, :]
        dt_wide = _dot_exact_lhs(dt_s[rows, :], expand[d])
        tot_wide = _dot_exact_lhs(jnp.broadcast_to(tot, (SUBLANES, LANES)), expand[d])[0:1, :]
        bmat = xbc_s[rows, D_SSD:D_SSD + LANES]
        cmat = xbc_s[rows, D_SSD + LANES:D_SSD + 2 * LANES]
        b_bf = bmat.astype(BF16)
        c_grp = [jnp.where(first_group_lanes, cmat, 0.0).astype(BF16),
                 jnp.where(first_group_lanes, 0.0, cmat).astype(BF16)]
        gmat = [_dot_nt(c_grp[g], b_bf) for g in range(G_SSD)]
        st_bf = st_s[d].astype(BF16)
        y_inter = [_dot(c_grp[g], st_bf) for g in range(G_SSD)]
        return dict(rows=rows, d=d, cs=cs, cst=cs.T, tot=tot, dt_wide=dt_wide, tot_wide=tot_wide,
                    gmat=gmat, y_inter=y_inter, bt=bmat.T.astype(BF16))

    def stage_b(t):
        rows, d, cs, cst, tot, dt_wide = t["rows"], t["d"], t["cs"], t["cst"], t["tot"], t["dt_wide"]
        gmat, y_inter, keep = t["gmat"], t["y_inter"], keeps[t["d"]]
        vte = []
        for pr in range(H_SSD // 2):
            g = pr // 2
            ia = d * H_SSD + 2 * pr
            ib = ia + 1
            pair = slice(pr * LANES, (pr + 1) * LANES)
            v = xbc_s[rows, pair] * dt_wide[:, pair]
            va = jnp.where(first_group_lanes, v, 0.0).astype(BF16)
            vb = jnp.where(first_group_lanes, 0.0, v).astype(BF16)
            col_a = jnp.broadcast_to(cs[:, ia:ia + 1], (CHUNK, LANES))
            col_b = jnp.broadcast_to(cs[:, ib:ib + 1], (CHUNK, LANES))
            seg_a = col_a - cst[ia:ia + 1, :]
            seg_b = col_b - cst[ib:ib + 1, :]
            s_a = (gmat[g] * jnp.exp(jnp.where(keep, seg_a, -jnp.inf))).astype(BF16)
            s_b = (gmat[g] * jnp.exp(jnp.where(keep, seg_b, -jnp.inf))).astype(BF16)
            y_intra = _dot(s_a, va) + _dot(s_b, vb)
            col_pair = jnp.where(first_group_lanes, col_a, col_b)
            tot_pair = jnp.where(half_row, tot[:, ia:ia + 1], tot[:, ib:ib + 1])
            lo = (pr % 2) * LANES
            y_pair = y_intra + y_inter[g][:, lo:lo + LANES] * jnp.exp(col_pair)
            y_s[rows, pair] += y_pair
            vte.append((v * jnp.exp(tot_pair - col_pair)).astype(BF16))

        bt = t["bt"]
        new0 = _dot(bt, jnp.concatenate([vte[0], vte[1]], axis=1))
        new1 = _dot(bt, jnp.concatenate([vte[2], vte[3]], axis=1))
        new = jnp.where(first_group_rows, new0, new1)
        cd = jnp.exp(t["tot_wide"])
        half_w = 4 * P_SSD
        decay = jnp.where(first_group_rows, cd[:, 0:half_w], cd[:, half_w:2 * half_w])
        st_s[d] = st_s[d] * decay + new

    tasks = [((c if d == 0 else nc - 1 - c) * CHUNK, d) for c in range(nc) for d in range(2)]
    yield
    pending = stage_a(*tasks[0])
    for nxt in tasks[1:]:
        yield
        upcoming = stage_a(*nxt)
        yield
        stage_b(pending)
        pending = upcoming
    yield
    stage_b(pending)
    yield

    if hout_ref is not None:
        for d in range(2):
            for h in range(H_SSD):
                g, hh = divmod(h, H_SSD // G_SSD)
                hout_ref[d, h] = st_s[d, g * N_SSD:(g + 1) * N_SSD, hh * P_SSD:(hh + 1) * P_SSD]

    y = y_s[...] * _silu(u_s[:, 0:D_SSD])
    y_ref[...] = _rms(y, nw_ref[...])


def _layer_spec(layer, *tail):
    zeros = (0,) * len(tail)
    return pl.BlockSpec((None,) + tail, lambda *_: (layer,) + zeros)


def _state_out(nb, layer, prev, *tail, group=1):
    zeros = (0,) * len(tail)
    if prev is None:
        assert layer == 0
        spec = pl.BlockSpec((group, DEPTH) + tail, lambda b: (b, 0) + zeros)
    else:
        spec = pl.BlockSpec((group, None) + tail, lambda b: (b, layer) + zeros)
    shape = jax.ShapeDtypeStruct((nb, DEPTH) + tail, F32)
    return spec, shape, ([] if prev is None else [prev])


def _own_block(out_ref, creates):
    if out_ref is None or not creates:
        return out_ref
    out_ref[:, 1:] = jnp.zeros((out_ref.shape[0], DEPTH - 1) + out_ref.shape[2:], out_ref.dtype)
    return out_ref.at[:, 0]


def _seq_group(seq_len, shared_cond):
    return max(1, GROUP_TOKENS // seq_len) if shared_cond else 1


def _seq_inputs(x, mod, seq_len, shared_cond, layer, n1w, w, width, group=1):
    nb = x.shape[0] // seq_len
    assert nb % group == 0 and (shared_cond or group == 1)
    w_all, w_col = w
    specs = [pl.BlockSpec((group, seq_len, D_MODEL), lambda b: (b, 0, 0)),
             _mod_spec(layer, None if shared_cond else 1),
             _layer_spec(layer, 1, D_MODEL),
             pl.BlockSpec((None, D_MODEL, width), lambda b: (layer, 0, w_col))]
    return nb, specs, [x.reshape(nb, seq_len, D_MODEL), mod, n1w, w_all]


def _ssd(x, mod, seq_len, shared_cond, h0, layer, prev_states, n1w, w, cw, cb, dtb, alog, dsk, nw):
    grp = _seq_group(seq_len, shared_cond)
    nb, in_specs, args = _seq_inputs(x, mod, seq_len, shared_cond, layer, n1w, w, U_SSD, grp)
    has_h0 = h0 is not None
    if has_h0:
        in_specs.append(pl.BlockSpec((grp, 1, 2, H_SSD, N_SSD, P_SSD), lambda b: (b, layer, 0, 0, 0, 0)))
        args.append(h0)
    in_specs += [
        _layer_spec(layer, 8, CONV_DIM),
        _layer_spec(layer, 1, CONV_DIM),
        _layer_spec(layer, 1, LANES),
        _layer_spec(layer, 1, LANES),
        _layer_spec(layer, 1, D_SSD),
        _layer_spec(layer, 1, D_SSD),
    ]
    args += [cw, cb, dtb, alog, dsk, nw]
    out_specs = [pl.BlockSpec((grp, seq_len, D_SSD), lambda b: (b, 0, 0))]
    out_shape = [jax.ShapeDtypeStruct((nb, seq_len, D_SSD), F32)]
    aliases = {}
    if not has_h0:
        spec, shape, prev = _state_out(nb, layer, prev_states, 2, H_SSD, N_SSD, P_SSD, group=grp)
        out_specs.append(spec)
        out_shape.append(shape)
        if prev:
            aliases = {len(args): 1}
            in_specs.append(pl.BlockSpec(memory_space=pl.ANY))
            args += prev
    outs = pl.pallas_call(
        functools.partial(_ssd_kernel, seq_len=seq_len, has_h0=has_h0, has_prev=bool(aliases), group=grp),
        grid=(nb // grp,),
        in_specs=in_specs,
        out_specs=out_specs,
        out_shape=out_shape,
        input_output_aliases=aliases,
        scratch_shapes=[
            pltpu.VMEM((grp, seq_len, U_SSD), F32),
            pltpu.VMEM((grp, seq_len + 2 * HALO, CONV_DIM), BF16),
            pltpu.VMEM((grp, seq_len, CONV_DIM), F32),
            pltpu.VMEM((grp, seq_len, LANES), F32),
            pltpu.VMEM((grp, seq_len, LANES), F32),
            pltpu.VMEM((grp, seq_len, D_SSD), F32),
            pltpu.VMEM((grp, 2, 2 * N_SSD, 4 * P_SSD), F32),
        ],
        compiler_params=pltpu.CompilerParams(
            dimension_semantics=("arbitrary",), vmem_limit_bytes=VMEM_LIMIT),
        name="ssd",
    )(*args)
    return outs[0].reshape(nb * seq_len, D_SSD), (None if has_h0 else outs[1])


def _ret_kernel(*refs, seq_len, has_h0, has_prev, group):
    if has_h0:
        hout_ref = None
        x_ref, mod_ref, n1w_ref, w_ref, h0_ref, dl_ref, gnw_ref, y_ref, *scratch = refs
    else:
        h0_ref = None
        refs = refs[:6] + refs[6 + has_prev:]
        x_ref, mod_ref, n1w_ref, w_ref, dl_ref, gnw_ref, y_ref, hout_ref, *scratch = refs
        hout_ref = _own_block(hout_ref, not has_prev)
    _interleave(
        _ret_seq(x_ref.at[s], mod_ref, n1w_ref, w_ref, None if h0_ref is None else h0_ref.at[s, 0],
                 dl_ref, gnw_ref, y_ref.at[s], None if hout_ref is None else hout_ref.at[s],
                 *[r.at[s] for r in scratch], seq_len=seq_len)
        for s in range(group))


def _ret_seq(x_ref, mod_ref, n1w_ref, w_ref, h0_ref, dl_ref, gnw_ref, y_ref, hout_ref, u_s, y_s, st_s,
             *, seq_len):
    has_h0 = h0_ref is not None
    yield from _project_in(x_ref, mod_ref, n1w_ref, w_ref, u_s)
    L = seq_len
    nc = L // CHUNK
    npair = H_RET // 2
    nq = H_RET * DK_RET

    lg_row = -_softplus(-dl_ref[...])
    ri = lax.broadcasted_iota(jnp.int32, (CHUNK, CHUNK), 0)
    ci = lax.broadcasted_iota(jnp.int32, (CHUNK, CHUNK), 1)
    rif = ri.astype(F32)
    half = _lane_half((CHUNK, LANES))
    block_diag = (ri < DK_RET) == (ci < DV_RET)

    if has_h0:
        st_s[...] = jnp.zeros(st_s.shape, F32)
        for d in range(2):
            for h in range(H_RET):
                pr, hh = divmod(h, 2)
                st_s[d, pr, hh * DK_RET:(hh + 1) * DK_RET, hh * DV_RET:(hh + 1) * DV_RET] = h0_ref[d, h]
    else:
        st_s[...] = jnp.zeros(st_s.shape, F32)
    y_s[...] = jnp.zeros(y_s.shape, F32)

    consts = {}
    for d in range(2):
        keep = (ri >= ci) if d == 0 else (ri <= ci)
        dist = (ri - ci).astype(F32) if d == 0 else (ci - ri).astype(F32)
        for pr in range(npair):
            ia = d * H_RET + 2 * pr
            lg_a = lg_row[:, ia:ia + 1]
            lg_b = lg_row[:, ia + 1:ia + 2]
            lg_pair = jnp.where(half, lg_a, lg_b)
            if d == 0:
                in_scale = jnp.exp((rif + 1.0) * lg_pair)
                to_end = jnp.exp((CHUNK - 1.0 - rif) * lg_pair)
            else:
                in_scale = jnp.exp((CHUNK - rif) * lg_pair)
                to_end = jnp.exp(rif * lg_pair)
            consts[d, pr] = dict(
                dec_a=jnp.exp(jnp.where(keep, dist * lg_a, -jnp.inf)),
                dec_b=jnp.exp(jnp.where(keep, dist * lg_b, -jnp.inf)),
                in_scale=in_scale, to_end=to_end, cd=jnp.exp(float(CHUNK) * lg_pair))

    def stage_a(start, d, pr):
        cst = consts[d, pr]
        rows = pl.ds(start, CHUNK)
        lanes = slice(pr * LANES, (pr + 1) * LANES)
        q = u_s[rows, lanes]
        k = u_s[rows, nq + pr * LANES:nq + (pr + 1) * LANES] * (DK_RET ** -0.5)
        v = u_s[rows, 2 * nq + pr * LANES:2 * nq + (pr + 1) * LANES]
        k_bf = k.astype(BF16)
        qk_a = _dot_nt(jnp.where(half, q, 0.0).astype(BF16), k_bf)
        qk_b = _dot_nt(jnp.where(half, 0.0, q).astype(BF16), k_bf)
        st = st_s[d, pr]
        y_inter = _dot(q.astype(BF16), st.astype(BF16)) * cst["in_scale"]
        new = _dot(k.T.astype(BF16), (v * cst["to_end"]).astype(BF16))
        st_s[d, pr] = st * cst["cd"] + jnp.where(block_diag, new, 0.0)
        return dict(rows=rows, lanes=lanes, cst=cst, qk_a=qk_a, qk_b=qk_b, y_inter=y_inter,
                    va=jnp.where(half, v, 0.0).astype(BF16), vb=jnp.where(half, 0.0, v).astype(BF16))

    def stage_b(t):
        s_a = (t["qk_a"] * t["cst"]["dec_a"]).astype(BF16)
        s_b = (t["qk_b"] * t["cst"]["dec_b"]).astype(BF16)
        y_s[t["rows"], t["lanes"]] += _dot(s_a, t["va"]) + _dot(s_b, t["vb"]) + t["y_inter"]

    tasks = [((c if d == 0 else nc - 1 - c) * CHUNK, d, pr)
             for c in range(nc) for d in range(2) for pr in range(npair)]
    yield
    pending = stage_a(*tasks[0])
    for nxt in tasks[1:]:
        yield
        upcoming = stage_a(*nxt)
        yield
        stage_b(pending)
        pending = upcoming
    yield
    stage_b(pending)
    yield

    if hout_ref is not None:
        for d in range(2):
            for h in range(H_RET):
                pr, hh = divmod(h, 2)
                hout_ref[d, h] = st_s[d, pr, hh * DK_RET:(hh + 1) * DK_RET, hh * DV_RET:(hh + 1) * DV_RET]

    half_l = _lane_half((L, LANES))
    for pr in range(npair):
        lanes = slice(pr * LANES, (pr + 1) * LANES)
        o = y_s[:, lanes]
        inv = 1.0 / DV_RET
        sum_a = jnp.sum(jnp.where(half_l, o, 0.0), axis=-1, keepdims=True)
        sum_all = jnp.sum(o, axis=-1, keepdims=True)
        mu = jnp.where(half_l, sum_a, sum_all - sum_a) * inv
        c = o - mu
        c2 = c * c
        sq_a = jnp.sum(jnp.where(half_l, c2, 0.0), axis=-1, keepdims=True)
        sq_all = jnp.sum(c2, axis=-1, keepdims=True)
        var = jnp.where(half_l, sq_a, sq_all - sq_a) * inv
        on = c * lax.rsqrt(var + EPS) * gnw_ref[:, lanes]
        gate = u_s[:, 2 * nq + D_RET + pr * LANES:2 * nq + D_RET + (pr + 1) * LANES]
        y_ref[:, lanes] = _silu(gate) * on
        yield


def _ret(x, mod, seq_len, shared_cond, h0, layer, prev_states, n1w, w, dl, gnw):
    grp = _seq_group(seq_len, shared_cond)
    nb, in_specs, args = _seq_inputs(x, mod, seq_len, shared_cond, layer, n1w, w, RET_PROJ, grp)
    has_h0 = h0 is not None
    if has_h0:
        in_specs.append(pl.BlockSpec((grp, 1, 2, H_RET, DK_RET, DV_RET), lambda b: (b, layer, 0, 0, 0, 0)))
        args.append(h0)
    in_specs += [_layer_spec(layer, 1, LANES), _layer_spec(layer, 1, D_RET)]
    args += [dl, gnw]
    out_specs = [pl.BlockSpec((grp, seq_len, D_RET), lambda b: (b, 0, 0))]
    out_shape = [jax.ShapeDtypeStruct((nb, seq_len, D_RET), F32)]
    aliases = {}
    if not has_h0:
        spec, shape, prev = _state_out(nb, layer, prev_states, 2, H_RET, DK_RET, DV_RET, group=grp)
        out_specs.append(spec)
        out_shape.append(shape)
        if prev:
            aliases = {len(args): 1}
            in_specs.append(pl.BlockSpec(memory_space=pl.ANY))
            args += prev
    outs = pl.pallas_call(
        functools.partial(_ret_kernel, seq_len=seq_len, has_h0=has_h0, has_prev=bool(aliases), group=grp),
        grid=(nb // grp,),
        in_specs=in_specs,
        out_specs=out_specs,
        out_shape=out_shape,
        input_output_aliases=aliases,
        scratch_shapes=[
            pltpu.VMEM((grp, seq_len, RET_PROJ), F32),
            pltpu.VMEM((grp, seq_len, D_RET), F32),
            pltpu.VMEM((grp, 2, H_RET // 2, 2 * DK_RET, 2 * DV_RET), F32),
        ],
        compiler_params=pltpu.CompilerParams(
            dimension_semantics=("arbitrary",), vmem_limit_bytes=VMEM_LIMIT),
        name="retention",
    )(*args)
    return outs[0].reshape(nb * seq_len, D_RET), (None if has_h0 else outs[1])


def _rot_matrix():
    r = lax.broadcasted_iota(jnp.int32, (LANES, LANES), 0)
    l = lax.broadcasted_iota(jnp.int32, (LANES, LANES), 1)
    even = (l // 8) % 2 == 0
    rope = l < ROPE_DIM
    p = jnp.where(rope & even & (r == l + 8), -1.0, 0.0) + jnp.where(rope & ~even & (r == l - 8), 1.0, 0.0)
    return p.astype(BF16)


def _mla_kernel(*refs, seq_len, past_len, n_prev, group):
    if past_len:
        (x_ref, mod_ref, n1w_ref, w_ref, pckv_ref, pkr_ref, cos_ref, sin_ref, qnw_ref, kvnw_ref,
         wq_ref, wqr_ref, wk_ref, wv_ref, o_ref, *scratch) = refs
        ckv_ref = kr_ref = None
    else:
        refs = refs[:9] + refs[9 + n_prev:]
        (x_ref, mod_ref, n1w_ref, w_ref, qnw_ref, kvnw_ref, wq_ref, wk_ref, wv_ref,
         o_ref, ckv_ref, kr_ref, *scratch) = refs
        pckv_ref = pkr_ref = cos_ref = sin_ref = wqr_ref = None
        ckv_ref = _own_block(ckv_ref, n_prev == 0)
        kr_ref = _own_block(kr_ref, n_prev == 0)
    at = lambda ref, *idx: None if ref is None else ref.at[idx]
    _interleave(
        _mla_seq(x_ref.at[s], mod_ref, n1w_ref, w_ref, at(pckv_ref, s, 0), at(pkr_ref, s, 0), cos_ref, sin_ref,
                 qnw_ref, kvnw_ref, wq_ref, wqr_ref, wk_ref, wv_ref, o_ref.at[s], at(ckv_ref, s), at(kr_ref, s),
                 *[r.at[s] for r in scratch], seq_len=seq_len, past_len=past_len)
        for s in range(group))


def _mla_seq(x_ref, mod_ref, n1w_ref, w_ref, pckv_ref, pkr_ref, cos_ref, sin_ref, qnw_ref, kvnw_ref,
             wq_ref, wqr_ref, wk_ref, wv_ref, o_ref, ckv_ref, kr_ref, u_s, q_s, k_s, v_s, ckv_s, kr_s,
             *, seq_len, past_len):
    yield from _project_in(x_ref, mod_ref, n1w_ref, w_ref, u_s)
    L = seq_len
    S = past_len + L
    nqb = L // CHUNK
    qscale = (NOPE_DIM + ROPE_DIM) ** -0.5 * math.log2(math.e)

    q_c = _rms(u_s[:, 0:Q_RANK], qnw_ref[...]).astype(BF16)
    ckv = _rms(u_s[:, Q_RANK:Q_RANK + KV_RANK], kvnw_ref[...])
    kr = u_s[:, Q_RANK + KV_RANK:U_MLA]
    if past_len:
        cos = cos_ref[...]
        sin = sin_ref[...]
        ckv_s[0:past_len, :] = pckv_ref[...]
        kr_s[0:past_len, :] = jnp.zeros((past_len, LANES), F32)
        kr_s[0:past_len, 0:ROPE_DIM] = pkr_ref[...]
        kr_s[past_len:S, :] = kr * cos + _dot_exact_lhs(kr, _rot_matrix()) * sin
    else:
        ckv_ref[...] = ckv
        kr_ref[...] = kr[:, 0:ROPE_DIM]
        kr_s[...] = kr
    ckv_s[past_len:S, :] = ckv

    ckv_all = ckv_s[...].astype(BF16)
    kr_all = kr_s[...]
    half_s = _lane_half((S, LANES))
    q_all = _dot(q_c, wq_ref[...])
    k_all = _dot(ckv_all, wk_ref[...])
    v_all = _dot(ckv_all, wv_ref[...])
    if past_len:
        qr_all = _dot(q_c, wqr_ref[...])
    for h in range(H_MLA):
        slab = slice(h * LANES, (h + 1) * LANES)
        qh = q_all[:, slab]
        if past_len:
            qh = qh * cos + qr_all[:, slab] * sin
        q_s[h] = (qh * qscale).astype(BF16)
        k_s[h] = (k_all[:, slab] + kr_all).astype(BF16)
    for pr in range(H_MLA // 2):
        vp = v_all[:, pr * LANES:(pr + 1) * LANES]
        v_s[2 * pr] = jnp.where(half_s, vp, 0.0).astype(BF16)
        v_s[2 * pr + 1] = jnp.where(half_s, 0.0, vp).astype(BF16)

    def scores(t):
        qb, h = divmod(t, H_MLA)
        return _dot_nt(q_s[h, qb * CHUNK:(qb + 1) * CHUNK, :], k_s[h])

    ntask = nqb * H_MLA
    yield
    s_next = scores(0)
    first = None
    for t in range(ntask):
        qb, h = divmod(t, H_MLA)
        s = s_next
        if t + 1 < ntask:
            s_next = scores(t + 1)
        e = jnp.exp2(s - jnp.max(s, axis=-1, keepdims=True))
        inv = 1.0 / jnp.sum(e, axis=-1, keepdims=True)
        o = _dot(e.astype(BF16), v_s[h]) * inv
        if h % 2 == 0:
            first = o
        else:
            pr = h // 2
            o_ref[qb * CHUNK:(qb + 1) * CHUNK, pr * LANES:(pr + 1) * LANES] = first + o
        yield


def _mla(x, mod, seq_len, shared_cond, cache, layer, rope, prev_cache, n1w, w, qnw, kvnw, wq, wqr, wk, wv):
    grp = _seq_group(seq_len, shared_cond)
    nb, in_specs, args = _seq_inputs(x, mod, seq_len, shared_cond, layer, n1w, w, U_MLA, grp)
    past_len = cache[0].shape[2] if cache is not None else 0
    s_len = past_len + seq_len
    full2 = lambda b: (0, 0)
    if past_len:
        in_specs += [
            pl.BlockSpec((grp, 1, past_len, KV_RANK), lambda b: (b, layer, 0, 0)),
            pl.BlockSpec((grp, 1, past_len, ROPE_DIM), lambda b: (b, layer, 0, 0)),
            pl.BlockSpec((seq_len, LANES), full2),
            pl.BlockSpec((seq_len, LANES), full2),
        ]
        args += [cache[0], cache[1], rope[0], rope[1]]
    in_specs += [_layer_spec(layer, 1, Q_RANK), _layer_spec(layer, 1, KV_RANK),
                 _layer_spec(layer, Q_RANK, H_MLA * LANES)]
    args += [qnw, kvnw, wq]
    if past_len:
        in_specs.append(_layer_spec(layer, Q_RANK, H_MLA * LANES))
        args.append(wqr)
    in_specs += [_layer_spec(layer, KV_RANK, H_MLA * LANES), _layer_spec(layer, KV_RANK, D_MLA)]
    args += [wk, wv]
    out_specs = [pl.BlockSpec((grp, seq_len, D_MLA), lambda b: (b, 0, 0))]
    out_shape = [jax.ShapeDtypeStruct((nb, seq_len, D_MLA), F32)]
    aliases = {}
    if not past_len:
        for i, width in enumerate((KV_RANK, ROPE_DIM)):
            spec, shape, prev = _state_out(nb, layer, None if prev_cache is None else prev_cache[i],
                                           seq_len, width, group=grp)
            out_specs.append(spec)
            out_shape.append(shape)
            if prev:
                aliases[len(args)] = 1 + i
                in_specs.append(pl.BlockSpec(memory_space=pl.ANY))
                args += prev
    outs = pl.pallas_call(
        functools.partial(_mla_kernel, seq_len=seq_len, past_len=past_len, n_prev=len(aliases), group=grp),
        grid=(nb // grp,),
        in_specs=in_specs,
        out_specs=out_specs,
        out_shape=out_shape,
        input_output_aliases=aliases,
        scratch_shapes=[
            pltpu.VMEM((grp, seq_len, U_MLA), F32),
            pltpu.VMEM((grp, H_MLA, seq_len, LANES), BF16),
            pltpu.VMEM((grp, H_MLA, s_len, LANES), BF16),
            pltpu.VMEM((grp, H_MLA, s_len, LANES), BF16),
            pltpu.VMEM((grp, s_len, KV_RANK), F32),
            pltpu.VMEM((grp, s_len, LANES), F32),
        ],
        compiler_params=pltpu.CompilerParams(
            dimension_semantics=("arbitrary",), vmem_limit_bytes=VMEM_LIMIT),
        name="mla",
    )(*args)
    o = outs[0].reshape(nb * seq_len, D_MLA)
    return (o,) + tuple(outs[1:])


def _out_ffn_kernel(x_ref, mssd_ref, mret_ref, mmla_ref, mod_ref, wos_ref, wor_ref, wom_ref, n2w_ref,
                    w1_ref, w2_ref, fnw_ref, o_ref, *, final):
    mix = (_dot(mssd_ref[...].astype(BF16), wos_ref[...])
           + _dot(mret_ref[...].astype(BF16), wor_ref[...])
           + _dot(mmla_ref[...].astype(BF16), wom_ref[...]))
    g1 = mod_ref[0, :, 2 * D_MODEL:3 * D_MODEL]
    sh2 = mod_ref[0, :, 3 * D_MODEL:4 * D_MODEL]
    sc2 = mod_ref[0, :, 4 * D_MODEL:5 * D_MODEL]
    g2 = mod_ref[0, :, 5 * D_MODEL:6 * D_MODEL]
    x1 = x_ref[...] + g1 * mix
    h2 = (_rms(x1, n2w_ref[...]) * (1.0 + sc2) + sh2).astype(BF16)

    def up(c):
        lo, hi = FF_SPLITS[c]
        return _dot(h2, w1_ref[:, lo:hi]), _dot(h2, w1_ref[:, D_FF + lo:D_FF + hi])

    acc = None
    nxt = up(0)
    for c, (lo, hi) in enumerate(FF_SPLITS):
        a, gt = nxt
        if c + 1 < len(FF_SPLITS):
            nxt = up(c + 1)
        down = _dot((_silu(a) * gt).astype(BF16), w2_ref[lo:hi, :])
        acc = down if acc is None else acc + down
    out = x1 + g2 * acc
    if final:
        out = _rms(out, fnw_ref[...])
    o_ref[...] = out


def _out_ffn(x, mssd, mret, mmla, mod, seq_len, shared_cond, layer, wo, n2w, w1, w2, fnw, final):
    t = x.shape[0]
    tm = _token_tile(seq_len, shared_cond)
    tok = lambda i: (i, 0)
    ret_blk = D_SSD // D_RET
    once = pl.Buffered(1)
    return pl.pallas_call(
        functools.partial(_out_ffn_kernel, final=final),
        grid=(t // tm,),
        in_specs=[
            pl.BlockSpec((tm, D_MODEL), tok),
            pl.BlockSpec((tm, D_SSD), tok),
            pl.BlockSpec((tm, D_RET), tok),
            pl.BlockSpec((tm, D_MLA), tok),
            _mod_spec(layer, None if shared_cond else seq_len // tm),
            pl.BlockSpec((None, D_SSD, D_MODEL), lambda i: (layer, 0, 0), pipeline_mode=once),
            pl.BlockSpec((None, D_RET, D_MODEL), lambda i: (layer, ret_blk, 0), pipeline_mode=once),
            pl.BlockSpec((None, D_MLA, D_MODEL), lambda i: (layer, ret_blk + 1, 0), pipeline_mode=once),
            _layer_spec(layer, 1, D_MODEL),
            pl.BlockSpec((None, D_MODEL, 2 * D_FF), lambda i: (layer, 0, 0), pipeline_mode=once),
            pl.BlockSpec((None, D_FF, D_MODEL), lambda i: (layer, 0, 0), pipeline_mode=once),
            pl.BlockSpec((1, D_MODEL), lambda i: (0, 0)),
        ],
        out_specs=pl.BlockSpec((tm, D_MODEL), tok),
        out_shape=jax.ShapeDtypeStruct((t, D_MODEL), F32),
        compiler_params=pltpu.CompilerParams(
            dimension_semantics=("arbitrary",), vmem_limit_bytes=VMEM_LIMIT,
            allow_input_fusion=[False] * 5 + [True] * 3 + [False] + [True] * 2 + [False]),
        name="out_ffn",
    )(x, mssd, mret, mmla, mod, wo, wo, wo, n2w, w1, w2, fnw)


def _pad_lanes(a, width):
    return jnp.pad(a, [(0, 0)] * (a.ndim - 1) + [(0, width - a.shape[-1])])


def _rope_tables(n_tokens):
    n_rows = n_tokens // GRID_W
    row, col = jnp.meshgrid(jnp.arange(n_rows), jnp.arange(GRID_W), indexing='ij')
    row = row.reshape(-1).astype(F32)
    col = col.reshape(-1).astype(F32)
    half = ROPE_DIM // 2
    inv = ROPE_BASE ** (-jnp.arange(0, half, 2, dtype=F32) / half)
    ang_r = row[:, None] * inv
    ang_c = col[:, None] * inv
    ang = jnp.concatenate([ang_r, ang_r, ang_c, ang_c], axis=-1)
    cos = jnp.concatenate([jnp.cos(ang), jnp.ones((n_tokens, LANES - ROPE_DIM), F32)], axis=-1)
    sin = _pad_lanes(jnp.sin(ang), LANES)
    return cos, sin


def kernel(x_prompt, x_sample, c, state_ssd, state_ret, cache_mla_ckv, cache_mla_krope, c_ctx,
           w_ada, b_ada, norm1_w, w_in, ssd_conv_w, ssd_conv_b, ssd_dt_bias, ssd_A_log, ssd_D, ssd_norm_w,
           ret_decay_logit, ret_gn_w, mla_q_norm_w, mla_w_uq, mla_kv_norm_w, mla_w_ukv, w_out, norm2_w,
           ffn_w1, ffn_w2, final_norm_w):
    batch, seq, _ = x_prompt.shape
    dec_batch, dec_seq, _ = x_sample.shape

    ret0 = SSD_PROJ
    mla0 = SSD_PROJ + RET_PROJ
    w_ssd = (_pad_lanes(w_in[..., 0:SSD_PROJ], U_SSD).astype(BF16), 0)
    w_ret = (w_in[..., ret0:mla0].astype(BF16), 0)
    w_mla = (_pad_lanes(w_in[..., mla0:], U_MLA).astype(BF16), 0)
    uq = mla_w_uq.reshape(DEPTH, Q_RANK, H_MLA, NOPE_DIM + ROPE_DIM).transpose(0, 2, 1, 3)
    def heads_side_by_side(w):
        return w.transpose(0, 2, 1, 3).reshape(DEPTH, w.shape[2], w.shape[1] * LANES).astype(BF16)

    wq = heads_side_by_side(_pad_lanes(jnp.concatenate([uq[..., NOPE_DIM:], uq[..., :NOPE_DIM]], axis=-1), LANES))
    qr = uq[..., NOPE_DIM:].reshape(DEPTH, H_MLA, Q_RANK, 2, 2, ROPE_DIM // 4)
    wq_rot = jnp.stack([-qr[..., 1, :], qr[..., 0, :]], axis=-2).reshape(DEPTH, H_MLA, Q_RANK, ROPE_DIM)
    wq_rot = heads_side_by_side(_pad_lanes(wq_rot, LANES))
    ukv = mla_w_ukv.reshape(DEPTH, KV_RANK, H_MLA, NOPE_DIM + V_DIM).transpose(0, 2, 1, 3)
    wk = heads_side_by_side(
        jnp.pad(ukv[..., :NOPE_DIM], [(0, 0)] * 3 + [(ROPE_DIM, LANES - ROPE_DIM - NOPE_DIM)]))
    wv = ukv[..., NOPE_DIM:].transpose(0, 2, 1, 3).reshape(DEPTH, KV_RANK, D_MLA).astype(BF16)
    wo = w_out.astype(BF16)
    w1 = ffn_w1.astype(BF16)
    w2 = ffn_w2.astype(BF16)
    conv_w = jnp.pad(ssd_conv_w, [(0, 0), (0, 8 - CONV_W), (0, 0)])
    conv_b = ssd_conv_b.reshape(DEPTH, 1, CONV_DIM)
    dt_bias = _pad_lanes(ssd_dt_bias.reshape(DEPTH, 1, 2 * H_SSD), LANES)
    a_log = _pad_lanes(ssd_A_log.reshape(DEPTH, 1, 2 * H_SSD), LANES)
    d_skip = jnp.repeat(ssd_D, P_SSD, axis=-1).reshape(DEPTH, 1, D_SSD)
    ssd_nw = ssd_norm_w.reshape(DEPTH, 1, D_SSD)
    decay_logit = _pad_lanes(ret_decay_logit.reshape(DEPTH, 1, 2 * H_RET), LANES)
    gn_w = ret_gn_w.reshape(DEPTH, 1, D_RET)
    q_nw = mla_q_norm_w.reshape(DEPTH, 1, Q_RANK)
    kv_nw = mla_kv_norm_w.reshape(DEPTH, 1, KV_RANK)
    n1w = norm1_w.reshape(DEPTH, 1, D_MODEL)
    n2w = norm2_w.reshape(DEPTH, 1, D_MODEL)
    fnw = final_norm_w.reshape(1, D_MODEL)
    rope = _rope_tables(dec_seq)

    assert dec_batch < N_COND
    cond = jnp.concatenate([c_ctx[None, :], c, jnp.zeros((N_COND - 1 - dec_batch, D_MODEL), F32)], axis=0)
    mod = _modulation(cond, w_ada, b_ada).reshape(DEPTH * N_COND, 1, 6 * D_MODEL)

    def layer(i, x, seq_len, shared_cond, h0_ssd, h0_ret, cache, prev):
        final = i == DEPTH - 1
        y_ssd, s_ssd = _ssd(x, mod, seq_len, shared_cond, h0_ssd, i, prev[0], n1w, w_ssd,
                            conv_w, conv_b, dt_bias, a_log, d_skip, ssd_nw)
        y_ret, s_ret = _ret(x, mod, seq_len, shared_cond, h0_ret, i, prev[1], n1w, w_ret, decay_logit, gn_w)
        mla_out = _mla(x, mod, seq_len, shared_cond, cache, i, rope, prev[2], n1w, w_mla,
                       q_nw, kv_nw, wq, wq_rot, wk, wv)
        x = _out_ffn(x, y_ssd, y_ret, mla_out[0], mod, seq_len, shared_cond, i, wo, n2w, w1, w2, fnw, final)
        return x, (s_ssd, s_ret, mla_out[1:])

    xp = x_prompt.reshape(batch * seq, D_MODEL)
    stacked = (None, None, None)
    for i in range(DEPTH):
        xp, stacked = layer(i, xp, seq, True, None, None, None, stacked)

    xs = x_sample.reshape(dec_batch * dec_seq, D_MODEL)
    for i in range(DEPTH):
        xs, _ = layer(i, xs, dec_seq, False, state_ssd, state_ret, (cache_mla_ckv, cache_mla_krope),
                      (None, None, None))

    new_ssd, new_ret, (new_ckv, new_kr) = stacked
    return (xp.reshape(batch, seq, D_MODEL), xs.reshape(dec_batch, dec_seq, D_MODEL),
            new_ssd, new_ret, new_ckv, new_kr)
```
